```python
import math
import jax, jax.numpy as jnp
from jax import lax
import numpy as np

D_MODEL = 2048
BATCH = 4
SEQ = 2048
DEPTH = 4
DEC_BATCH = 8
DEC_SEQ = 8
PAST_LEN = 16384
PAGE_SIZE = 128

ML_HEADS = 4
ML_HEAD_DIM = D_MODEL // 8
ML_WIDTH = ML_HEADS * ML_HEAD_DIM
SB_HEADS = 8
SB_HEAD_DIM = D_MODEL // 16
SB_WIDTH = SB_HEADS * SB_HEAD_DIM
MIX_WIDTH = ML_WIDTH + SB_WIDTH
SB_BIAS_INIT = -8.0
IN_SIZES = (ML_WIDTH, ML_WIDTH, ML_WIDTH, ML_WIDTH, 2 * ML_HEADS, SB_WIDTH, SB_WIDTH, SB_WIDTH)
IN_SPLITS = tuple(int(s) for s in np.cumsum(IN_SIZES)[:-1])
N_IN = int(sum(IN_SIZES))
N_MEM = 256
X_HEADS = 4
X_HEAD_DIM = D_MODEL // 16
X_WIDTH = X_HEADS * X_HEAD_DIM
D_FF = 11 * D_MODEL // 4
CONV_W = 3
ML_CHUNK = 64
SB_BLOCK = 128
EPS = 1e-6
M_EMPTY = -1e30

kernel_name = 'hymba_mlstm_stickbreak_convffn_step'


def rmsnorm(x, g):
    xf = x.astype(jnp.float32)
    y = xf * lax.rsqrt(jnp.mean(xf * xf, axis=-1, keepdims=True) + EPS)
    return (y * g.astype(jnp.float32)).astype(x.dtype)


def head_rmsnorm(h, g):
    B, T, H, d = h.shape
    hf = h.astype(jnp.float32)
    hf = hf * lax.rsqrt(jnp.mean(hf * hf, axis=-1, keepdims=True) + EPS)
    return (hf * g.astype(jnp.float32).reshape(H, d)).reshape(B, T, H * d)


def mlstm_chunkwise(q, k, v, i_pre, f_pre, c0, n0, m0):
    B, T, H, d = q.shape
    L = math.gcd(T, ML_CHUNK)
    nc = T // L
    f32 = jnp.float32

    def chunks(a):
        return jnp.moveaxis(a.astype(f32).reshape((B, nc, L) + a.shape[2:]), 1, 0)

    causal = jnp.tril(jnp.ones((L, L), dtype=bool))

    def step(carry, inp):
        C, n, m = carry
        qb, kb, vb, ib, lfb = inp
        b = jnp.swapaxes(jnp.cumsum(lfb, axis=1), 1, 2)
        it = jnp.swapaxes(ib, 1, 2)
        logd = jnp.where(causal, b[..., :, None] - b[..., None, :] + it[..., None, :], -jnp.inf)
        g = b + m[..., None]
        mt = jnp.maximum(g, jnp.max(logd, axis=-1))
        s = jnp.einsum('blhd,bshd->bhls', qb, kb) * jnp.exp(logd - mt[..., None])
        gw = jnp.exp(g - mt)
        num = (jnp.einsum('bhls,bshe->blhe', s, vb)
               + jnp.swapaxes(gw, 1, 2)[..., None] * jnp.einsum('blhd,bhde->blhe', qb, C))
        den = jnp.sum(s, axis=-1) + gw * jnp.einsum('blhd,bhd->bhl', qb, n)
        den = jnp.maximum(jnp.abs(den), jnp.exp(-mt))
        h = num / jnp.swapaxes(den, 1, 2)[..., None]
        m_new = mt[..., -1]
        wk = jnp.exp(b[..., -1:] - b + it - m_new[..., None])
        decay = jnp.exp(g[..., -1] - m_new)
        c_new = decay[..., None, None] * C + jnp.einsum('bhs,bshd,bshe->bhde', wk, kb, vb)
        n_new = decay[..., None] * n + jnp.einsum('bhs,bshd->bhd', wk, kb)
        return (c_new, n_new, m_new), h

    xs = (chunks(q), chunks(k) * (d ** -0.5), chunks(v), chunks(i_pre),
          chunks(jax.nn.log_sigmoid(f_pre.astype(f32))))
    (c, n, m), h = lax.scan(step, (c0.astype(f32), n0.astype(f32), m0.astype(f32)), xs)
    h = jnp.moveaxis(h, 0, 1).reshape(B, T, H, d)
    return h, c, n, m


def stick_breaking(q, qpos, k, v, kpos, bias):
    B, T, H, d = q.shape
    Q = math.gcd(T, SB_BLOCK)
    nb = T // Q
    vf = v.astype(jnp.float32)
    bh = bias.astype(jnp.float32)[None, :, None, None]

    def block(args):
        qb, qp = args
        z = jnp.einsum('bqhd,bshd->bhqs', qb, k).astype(jnp.float32) * (d ** -0.5) + bh
        vis = kpos[None, :] < qp[:, None]
        log_keep = jnp.where(vis, jax.nn.log_sigmoid(-z), 0.0)
        after = lax.cumsum(log_keep, axis=3, reverse=True) - log_keep
        a = jnp.where(vis, jnp.exp(jax.nn.log_sigmoid(z) + after), 0.0)
        return jnp.einsum('bhqs,bshd->bqhd', a, vf)

    qb = jnp.moveaxis(q.reshape(B, nb, Q, H, d), 1, 0)
    out = lax.map(block, (qb, qpos.reshape(nb, Q)))
    return jnp.moveaxis(out, 0, 1).reshape(B, T, H, d)


def cross_attn(h, mk, mv, wq, wo):
    B, T, _ = h.shape
    q = (h @ wq).reshape(B, T, X_HEADS, X_HEAD_DIM)
    s = jnp.einsum('bthd,bmhd->bhtm', q, mk).astype(jnp.float32) * (X_HEAD_DIM ** -0.5)
    p = jax.nn.softmax(s, axis=-1)
    o = jnp.einsum('bhtm,bmhd->bthd', p.astype(mv.dtype), mv).reshape(B, T, X_WIDTH)
    return o @ wo


def mem_keys_values(mem, g_mem, wx_kv):
    B, M, _ = mem.shape
    mk, mv = jnp.split(rmsnorm(mem, g_mem) @ wx_kv, 2, axis=-1)
    return mk.reshape(B, M, X_HEADS, X_HEAD_DIM), mv.reshape(B, M, X_HEADS, X_HEAD_DIM)


def conv_ffn(h, buf, w_up, conv_w, conv_b, w_down):
    T = h.shape[1]
    a, b = jnp.split(h @ w_up, 2, axis=-1)
    a_ext = jnp.concatenate([buf.astype(a.dtype), a], axis=1)
    c = conv_b + sum(a_ext[:, j:j + T] * conv_w[j] for j in range(CONV_W))
    y = (jax.nn.silu(c) * b) @ w_down
    return y, a_ext[:, a_ext.shape[1] - (CONV_W - 1):]


def decoder_layer(x, pos0, past_k, past_v, c0, n0, m0, conv_buf, mem_k, mem_v,
                  g_mix, w_in, b_gates, b_sb, g_heads, w_out, g_cross, wx_q, wx_o,
                  g_ffn, w_up, conv_w, conv_b, w_down):
    B, T, _ = x.shape
    h = rmsnorm(x, g_mix)
    q_m, k_m, v_m, o_m, gates, q_s, k_s, v_s = jnp.split(h @ w_in, IN_SPLITS, axis=-1)
    i_pre, f_pre = jnp.split(gates + b_gates, 2, axis=-1)
    mh = lambda a: a.reshape(B, T, ML_HEADS, ML_HEAD_DIM)
    sh = lambda a: a.reshape(B, T, SB_HEADS, SB_HEAD_DIM)
    h_ml, c, n, m = mlstm_chunkwise(mh(q_m), mh(k_m), mh(v_m), i_pre, f_pre, c0, n0, m0)
    h_ml = jax.nn.sigmoid(mh(o_m).astype(jnp.float32)) * h_ml
    k_new, v_new = sh(k_s), sh(v_s)
    if past_k is None:
        k_all, v_all = k_new, v_new
    else:
        k_all = jnp.concatenate([past_k.astype(k_new.dtype), k_new], axis=1)
        v_all = jnp.concatenate([past_v.astype(v_new.dtype), v_new], axis=1)
    kpos = jnp.arange(k_all.shape[1])
    qpos = pos0 + jnp.arange(T)
    h_sb = stick_breaking(sh(q_s), qpos, k_all, v_all, kpos, b_sb)
    g_ml, g_sb = jnp.split(g_heads, [ML_WIDTH])
    mix = jnp.concatenate([head_rmsnorm(h_ml, g_ml), head_rmsnorm(h_sb, g_sb)], axis=-1)
    x = x + mix.astype(x.dtype) @ w_out
    x = x + cross_attn(rmsnorm(x, g_cross), mem_k, mem_v, wx_q, wx_o)
    f, new_buf = conv_ffn(rmsnorm(x, g_ffn), conv_buf, w_up, conv_w, conv_b, w_down)
    x = x + f
    dt = x.dtype
    return x, (k_new, v_new, c.astype(dt), n.astype(dt), m.astype(dt), new_buf)


def setup_inputs(seed: int = 0) -> dict:
    key = jax.random.key(seed)
    ks = iter(jax.random.split(key, 40))
    f32 = jnp.float32
    nrm = lambda shape, scale: jax.random.normal(next(ks), shape, f32) * scale
    gain = lambda shape: 1.0 + nrm(shape, 0.02)
    n_pages = PAST_LEN // PAGE_SIZE
    n_used = DEC_BATCH * n_pages
    n_pool = (n_used * 5) // 4
    page_table = jax.random.permutation(next(ks), n_pool)[:n_used].reshape(DEC_BATCH, n_pages).astype(jnp.int32)
    b_i = nrm((DEPTH, ML_HEADS), 0.1)
    b_f = jax.random.uniform(next(ks), (DEPTH, ML_HEADS), f32, minval=3.0, maxval=6.0)
    return {
        'x_prompt': nrm((BATCH, SEQ, D_MODEL), 1.0),
        'x_sample': nrm((DEC_BATCH, DEC_SEQ, D_MODEL), 1.0),
        'mem_prompt': nrm((BATCH, N_MEM, D_MODEL), 1.0),
        'cache_sb_k': nrm((DEPTH, n_pool, PAGE_SIZE, SB_HEADS, SB_HEAD_DIM), 1.0),
        'cache_sb_v': nrm((DEPTH, n_pool, PAGE_SIZE, SB_HEADS, SB_HEAD_DIM), 1.0),
        'state_mlstm_c': nrm((DEPTH, DEC_BATCH, ML_HEADS, ML_HEAD_DIM, ML_HEAD_DIM), 0.5),
        'state_mlstm_n': nrm((DEPTH, DEC_BATCH, ML_HEADS, ML_HEAD_DIM), 0.5),
        'state_mlstm_m': nrm((DEPTH, DEC_BATCH, ML_HEADS), 1.0),
        'state_ffn_conv': nrm((DEPTH, DEC_BATCH, CONV_W - 1, D_FF), 1.0),
        'cache_mem_k': nrm((DEPTH, DEC_BATCH, N_MEM, X_HEADS, X_HEAD_DIM), 1.0),
        'cache_mem_v': nrm((DEPTH, DEC_BATCH, N_MEM, X_HEADS, X_HEAD_DIM), 1.0),
        'page_table': page_table,
        'norm_mix': gain((DEPTH, D_MODEL)),
        'w_in': nrm((DEPTH, D_MODEL, N_IN), D_MODEL ** -0.5),
        'b_gates': jnp.concatenate([b_i, b_f], axis=-1),
        'b_sb': SB_BIAS_INIT + nrm((DEPTH, SB_HEADS), 0.1),
        'norm_heads': gain((DEPTH, MIX_WIDTH)),
        'w_out': nrm((DEPTH, MIX_WIDTH, D_MODEL), MIX_WIDTH ** -0.5),
        'norm_cross': gain((DEPTH, D_MODEL)),
        'norm_mem': gain((DEPTH, D_MODEL)),
        'wx_q': nrm((DEPTH, D_MODEL, X_WIDTH), D_MODEL ** -0.5),
        'wx_kv': nrm((DEPTH, D_MODEL, 2 * X_WIDTH), D_MODEL ** -0.5),
        'wx_o': nrm((DEPTH, X_WIDTH, D_MODEL), X_WIDTH ** -0.5),
        'norm_ffn': gain((DEPTH, D_MODEL)),
        'w_up': nrm((DEPTH, D_MODEL, 2 * D_FF), D_MODEL ** -0.5),
        'conv_w': nrm((DEPTH, CONV_W, D_FF), CONV_W ** -0.5),
        'conv_b': nrm((DEPTH, D_FF), 0.02),
        'w_down': nrm((DEPTH, D_FF, D_MODEL), D_FF ** -0.5),
        'norm_final': gain((D_MODEL,)),
    }


def reference(x_prompt, x_sample, mem_prompt, cache_sb_k, cache_sb_v, state_mlstm_c, state_mlstm_n,
              state_mlstm_m, state_ffn_conv, cache_mem_k, cache_mem_v, page_table,
              norm_mix, w_in, b_gates, b_sb, norm_heads, w_out, norm_cross, norm_mem, wx_q, wx_kv, wx_o,
              norm_ffn, w_up, conv_w, conv_b, w_down, norm_final):
    Bp = x_prompt.shape[0]
    Bs = x_sample.shape[0]
    past = page_table.shape[1] * PAGE_SIZE
    P = [[] for _ in range(8)]
    S = [[] for _ in range(6)]
    yp, ys = x_prompt, x_sample
    for l in range(DEPTH):
        w = (norm_mix[l], w_in[l], b_gates[l], b_sb[l], norm_heads[l], w_out[l], norm_cross[l], wx_q[l], wx_o[l],
             norm_ffn[l], w_up[l], conv_w[l], conv_b[l], w_down[l])
        mk_p, mv_p = mem_keys_values(mem_prompt, norm_mem[l], wx_kv[l])
        c0 = jnp.zeros((Bp, ML_HEADS, ML_HEAD_DIM, ML_HEAD_DIM), jnp.float32)
        n0 = jnp.zeros((Bp, ML_HEADS, ML_HEAD_DIM), jnp.float32)
        m0 = jnp.full((Bp, ML_HEADS), M_EMPTY, jnp.float32)
        buf0 = jnp.zeros((Bp, CONV_W - 1, D_FF), yp.dtype)
        yp, st_p = decoder_layer(yp, 0, None, None, c0, n0, m0, buf0, mk_p, mv_p, *w)
        for lst, a in zip(P, st_p + (mk_p, mv_p)):
            lst.append(a)
        pk = cache_sb_k[l][page_table].reshape(Bs, past, SB_HEADS, SB_HEAD_DIM)
        pv = cache_sb_v[l][page_table].reshape(Bs, past, SB_HEADS, SB_HEAD_DIM)
        ys, st_s = decoder_layer(ys, past, pk, pv, state_mlstm_c[l], state_mlstm_n[l], state_mlstm_m[l],
                                 state_ffn_conv[l], cache_mem_k[l], cache_mem_v[l], *w)
        for lst, a in zip(S, st_s):
            lst.append(a)
    y_prompt = rmsnorm(yp, norm_final)
    y_sample = rmsnorm(ys, norm_final)
    p_sb_k, p_sb_v, p_ml_c, p_ml_n, p_ml_m, p_conv, p_mem_k, p_mem_v = (jnp.stack(a) for a in P)
    s_sb_k, s_sb_v, s_ml_c, s_ml_n, s_ml_m, s_conv = (jnp.stack(a) for a in S)
    return (y_prompt, y_sample, p_sb_k, p_sb_v, p_ml_c, p_ml_n, p_ml_m, p_conv, p_mem_k, p_mem_v,
            s_sb_k, s_sb_v, s_ml_c, s_ml_n, s_ml_m, s_conv)
```

```python
import functools

import jax
import jax.numpy as jnp
from jax import lax
from jax.experimental import pallas as pl
from jax.experimental.pallas import tpu as pltpu

F32 = jnp.float32
BF16 = jnp.bfloat16

EPS = 1e-6
M_EMPTY = -1e30
ML_HEADS = 4
ML_DIM = 256
SB_HEADS = 8
SB_DIM = 128
X_HEADS = 4
X_DIM = 128
PAGE = 128
CONV_W = 3
ML_WIDTH = ML_HEADS * ML_DIM
SB_WIDTH = SB_HEADS * SB_DIM
X_WIDTH = X_HEADS * X_DIM

SUBLANES = 8
LANES = 128
MIB = 1024 * 1024
VMEM_LIMIT = 56 * MIB

COL_MQ, COL_MK, COL_MV, COL_MO = 0, 1, 2, 3
COL_SQ, COL_SK, COL_SV = 4, 5, 6
N_PROJ = 7 * ML_WIDTH

NT_DIMS = (((1,), (1,)), ((), ()))
TN_DIMS = (((0,), (0,)), ((), ()))


def _params(sem):
    return pltpu.CompilerParams(dimension_semantics=sem, vmem_limit_bytes=VMEM_LIMIT)


def _rmsnorm(xf, g):
    return xf * lax.rsqrt(jnp.mean(xf * xf, axis=-1, keepdims=True) + EPS) * g


def _log_sigmoid(x):
    return jnp.minimum(x, 0.0) - jnp.log1p(jnp.exp(-jnp.abs(x)))


def _split_dot(x, u):
    hi = x.astype(BF16)
    lo = (x - hi.astype(F32)).astype(BF16)
    return (jnp.dot(hi, u, preferred_element_type=F32) + jnp.dot(lo, u, preferred_element_type=F32))


def _norm_matmul_kernel(x_ref, g_ref, w_ref, o_ref, hn_ref):
    @pl.when(pl.program_id(1) == 0)
    def _():
        hn_ref[...] = _rmsnorm(x_ref[...], g_ref[...]).astype(BF16)

    o_ref[...] = jnp.dot(hn_ref[...], w_ref[...], preferred_element_type=F32)


def _norm_matmul(x, g, w, tm, tn):
    M, D = x.shape
    N = w.shape[1]
    assert M % tm == 0 and N % tn == 0
    return pl.pallas_call(
        _norm_matmul_kernel,
        grid=(M // tm, N // tn),
        in_specs=[pl.BlockSpec((tm, D), lambda i, j: (i, 0)),
                  pl.BlockSpec((1, D), lambda i, j: (0, 0)),
                  pl.BlockSpec((D, tn), lambda i, j: (0, j))],
        out_specs=pl.BlockSpec((tm, tn), lambda i, j: (i, j)),
        out_shape=jax.ShapeDtypeStruct((M, N), F32),
        scratch_shapes=[pltpu.VMEM((tm, D), BF16)],
        compiler_params=_params(("parallel", "arbitrary")),
        name="norm_matmul",
    )(x, g.reshape(1, D), w)


def _mlstm_kernel(q_ref, k_ref, v_ref, o_ref, gt_ref, bg_ref, gn_ref, *rest, L, has_init):
    if has_init:
        c0_ref, n0_ref, m0_ref, mix_ref, c_ref, n_ref, m_ref = rest
    else:
        mix_ref, c_ref, n_ref, m_ref = rest
    S = LANES
    d = ML_DIM
    h = pl.program_id(1)

    @pl.when(pl.program_id(2) == 0)
    def _():
        if has_init:
            c_ref[...] = c0_ref[...]
            n_ref[...] = n0_ref[...]
            m_ref[...] = m0_ref[...]
        else:
            c_ref[...] = jnp.zeros(c_ref.shape, F32)
            n_ref[...] = jnp.zeros(n_ref.shape, F32)
            m_ref[...] = jnp.full(m_ref.shape, M_EMPTY, F32)

    def pad(a):
        if L == S:
            return a
        return jnp.concatenate([a, jnp.zeros((S - L, a.shape[1]), a.dtype)], axis=0)

    gates = pad(gt_ref[...] + bg_ref[...])
    lane = lax.broadcasted_iota(jnp.int32, (S, LANES), 1)
    i_col = jnp.sum(jnp.where(lane == h, gates, 0.0), axis=-1, keepdims=True)
    f_col = jnp.sum(jnp.where(lane == h + ML_HEADS, gates, 0.0), axis=-1, keepdims=True)
    lf = _log_sigmoid(f_col)

    row = lax.broadcasted_iota(jnp.int32, (S, S), 0)
    col = lax.broadcasted_iota(jnp.int32, (S, S), 1)
    causal = col <= row
    b_b = jnp.dot(causal.astype(F32), jnp.broadcast_to(lf, (S, LANES)),
                  precision=lax.Precision.HIGHEST, preferred_element_type=F32)
    b_col = b_b[:, 0:1]
    r_row = (jnp.broadcast_to(i_col, (S, LANES)) - b_b).T[0:1, :]

    m_prev = m_ref[0]
    g_col = b_col + m_prev
    logd = b_col + r_row
    mt = jnp.maximum(g_col, jnp.max(jnp.where(causal, logd, -jnp.inf), axis=-1, keepdims=True))

    q = pad(q_ref[...])
    k = pad(k_ref[...]) * (d ** -0.5)
    v = pad(v_ref[...])
    qb, kb = q.astype(BF16), k.astype(BF16)
    s = lax.dot_general(qb, kb, NT_DIMS, preferred_element_type=F32)
    s = jnp.where(causal, s * jnp.exp(logd - mt), 0.0)
    gw = jnp.exp(g_col - mt)
    C = c_ref[0, 0]
    n_row = n_ref[0]
    num = (jnp.dot(s.astype(BF16), v.astype(BF16), preferred_element_type=F32)
           + gw * jnp.dot(qb, C.astype(BF16), preferred_element_type=F32))
    den = jnp.sum(s, axis=-1, keepdims=True) + gw * jnp.sum(q * n_row, axis=-1, keepdims=True)
    den = jnp.maximum(jnp.abs(den), jnp.exp(-mt))
    hcell = num / den

    m_new = mt[L - 1:L, :]
    valid = lax.broadcasted_iota(jnp.int32, (S, 1), 0) < L
    wk = jnp.where(valid, jnp.exp(b_col[L - 1:L, :] - b_col + i_col - m_new), 0.0)
    decay = jnp.exp(g_col[L - 1:L, :] - m_new)
    c_ref[0, 0] = decay * C + lax.dot_general(kb, (wk * v).astype(BF16), TN_DIMS, preferred_element_type=F32)
    n_ref[0] = decay * n_row + jnp.sum(wk * k, axis=0, keepdims=True)
    m_ref[0] = m_new

    hm = jax.nn.sigmoid(o_ref[...]) * hcell[:L]
    mix_ref[...] = _rmsnorm(hm, gn_ref[...]).astype(mix_ref.dtype)


def _mlstm(proj, gates, b_gates, g_ml, B, T, L, init, mix_dtype):
    nc = T // L
    H, d = ML_HEADS, ML_DIM
    has_init = init is not None

    def qkvo(colblk):
        return pl.BlockSpec((L, d), lambda b, h, c: (b * nc + c, colblk * H + h))

    state_specs = [pl.BlockSpec((1, 1, d, d), lambda b, h, c: (b, h, 0, 0)),
                   pl.BlockSpec((1, 1, d), lambda b, h, c: (b * H + h, 0, 0)),
                   pl.BlockSpec((1, 1, 1), lambda b, h, c: (b * H + h, 0, 0))]
    in_specs = [qkvo(COL_MQ), qkvo(COL_MK), qkvo(COL_MV), qkvo(COL_MO),
                pl.BlockSpec((L, LANES), lambda b, h, c: (b * nc + c, 0)),
                pl.BlockSpec((1, LANES), lambda b, h, c: (0, 0)),
                pl.BlockSpec((1, d), lambda b, h, c: (0, h))]
    args = [proj, proj, proj, proj, gates, b_gates, g_ml.reshape(1, ML_WIDTH)]
    if has_init:
        c0, n0, m0 = init
        in_specs += state_specs
        args += [c0, n0.reshape(B * H, 1, d), m0.reshape(B * H, 1, 1)]
    mix, c, n, m = pl.pallas_call(
        functools.partial(_mlstm_kernel, L=L, has_init=has_init),
        grid=(B, H, nc),
        in_specs=in_specs,
        out_specs=[pl.BlockSpec((L, d), lambda b, h, c: (b * nc + c, h))] + state_specs,
        out_shape=[jax.ShapeDtypeStruct((B * T, ML_WIDTH), mix_dtype),
                   jax.ShapeDtypeStruct((B, H, d, d), F32),
                   jax.ShapeDtypeStruct((B * H, 1, d), F32),
                   jax.ShapeDtypeStruct((B * H, 1, 1), F32)],
        compiler_params=_params(("parallel", "parallel", "arbitrary")),
        name="mlstm",
    )(*args)
    return mix, c, n.reshape(B, H, d), m.reshape(B, H)


def _sb_block(z, u, carry, mask):
    sp = jnp.maximum(z, 0.0) + jnp.log1p(jnp.exp(-jnp.abs(z)))
    lk = -sp
    if mask is not None:
        lk = jnp.where(mask, lk, 0.0)
    after = _split_dot(lk, u)
    a = jnp.exp((z - sp) + after + carry)
    if mask is not None:
        a = jnp.where(mask, a, 0.0)
    return a, carry + (after[:, 0:1] + lk[:, 0:1])


def _sb_prompt_kernel(bias_ref, q_ref, k_ref, v_ref, gn_ref, o_ref, kb_ref, vb_ref, z_ref, e_ref, *, TQ):
    h = pl.program_id(1)
    i = pl.program_id(2)

    @pl.when(i == 0)
    def _():
        kb_ref[...] = k_ref[...].astype(BF16)
        vb_ref[...] = v_ref[...].astype(BF16)

    bias = bias_ref[h]
    scale = SB_DIM ** -0.5
    qb = q_ref[...].astype(BF16)
    row = lax.broadcasted_iota(jnp.int32, (TQ, TQ), 0)
    col = lax.broadcasted_iota(jnp.int32, (TQ, TQ), 1)
    u = (row > col).astype(BF16)

    def rows(j):
        return pl.ds(pl.multiple_of(jnp.maximum(j, 0) * TQ, TQ), TQ)

    def logits(j):
        return lax.dot_general(qb, kb_ref[rows(j), :], NT_DIMS, preferred_element_type=F32) * scale + bias

    def exponent(z, mask):
        sp = jnp.maximum(z, 0.0) + jnp.log1p(jnp.exp(-jnp.abs(z)))
        lk = -sp
        if mask is not None:
            lk = jnp.where(mask, lk, 0.0)
        after = _split_dot(lk, u)
        e = (z - sp) + after
        if mask is not None:
            e = jnp.where(mask, e, M_EMPTY)
        return e, after[:, 0:1] + lk[:, 0:1]

    e0, tot0 = exponent(logits(i), col < row)
    e_ref[...] = e0
    z_ref[...] = logits(i - 1)

    def body(t, st):
        tot, carry, acc = st
        j = i - t
        a = jnp.exp(e_ref[...] + carry)
        acc = acc + jnp.dot(a.astype(BF16), vb_ref[rows(j), :], preferred_element_type=F32)
        e_new, tot_new = exponent(z_ref[...], None)
        e_ref[...] = e_new
        z_ref[...] = logits(j - 2)
        return tot_new, carry + tot, acc

    _, _, acc = lax.fori_loop(0, i + 1, body, (tot0, jnp.zeros((TQ, 1), F32), jnp.zeros((TQ, SB_DIM), F32)))
    o_ref[...] = _rmsnorm(acc, gn_ref[...]).astype(o_ref.dtype)


def _sb_prompt(proj, b_sb, g_sb, B, T):
    TQ = min(256, T)
    nq = T // TQ
    H, d = SB_HEADS, SB_DIM
    return pl.pallas_call(
        functools.partial(_sb_prompt_kernel, TQ=TQ),
        grid=(B, H, nq),
        in_specs=[pl.BlockSpec(memory_space=pltpu.SMEM),
                  pl.BlockSpec((TQ, d), lambda b, h, i: (b * nq + i, COL_SQ * H + h)),
                  pl.BlockSpec((T, d), lambda b, h, i: (b, COL_SK * H + h)),
                  pl.BlockSpec((T, d), lambda b, h, i: (b, COL_SV * H + h)),
                  pl.BlockSpec((1, d), lambda b, h, i: (0, h))],
        out_specs=pl.BlockSpec((TQ, d), lambda b, h, i: (b * nq + i, h)),
        out_shape=jax.ShapeDtypeStruct((B * T, SB_WIDTH), BF16),
        scratch_shapes=[pltpu.VMEM((T, d), BF16), pltpu.VMEM((T, d), BF16),
                        pltpu.VMEM((TQ, TQ), F32), pltpu.VMEM((TQ, TQ), F32)],
        compiler_params=_params(("parallel", "parallel", "arbitrary")),
        name="sb_prompt",
    )(b_sb, proj, proj, proj, g_sb.reshape(1, SB_WIDTH))


def _sb_sample_kernel(pt_ref, q_ref, kn_ref, vn_ref, bias_ref, gn_ref, *rest, T, G):
    k_refs, v_refs = rest[:G], rest[G:2 * G]
    o_ref, acc_ref, carry_ref = rest[2 * G:]
    H, d = SB_HEADS, SB_DIM
    R = H * T
    s_id = pl.program_id(1)
    scale = d ** -0.5
    row = lax.broadcasted_iota(jnp.int32, (PAGE, PAGE), 0)
    col = lax.broadcasted_iota(jnp.int32, (PAGE, PAGE), 1)
    u = (row > col).astype(BF16)
    qb = q_ref[...].astype(BF16)
    bias = bias_ref[...]

    def logits(keys_of_head):
        z = [lax.dot_general(qb[h * T:(h + 1) * T], keys_of_head(h), NT_DIMS, preferred_element_type=F32)
             for h in range(H)]
        return jnp.concatenate(z, axis=0) * scale + bias

    def weighted_values(a, values_of_head):
        ab = a.astype(BF16)
        o = [jnp.dot(ab[h * T:(h + 1) * T], values_of_head(h), preferred_element_type=F32) for h in range(H)]
        return jnp.concatenate(o, axis=0)

    @pl.when(s_id == 0)
    def _():
        zpad = jnp.zeros((PAGE - T, d), F32)
        new_rows = lambda ref: lambda h: jnp.concatenate([ref[h * T:(h + 1) * T, :], zpad], axis=0).astype(BF16)
        z = logits(new_rows(kn_ref))
        mask = (lax.broadcasted_iota(jnp.int32, (R, PAGE), 1) < lax.broadcasted_iota(jnp.int32, (R, PAGE), 0) % T)
        a, carry = _sb_block(z, u, jnp.zeros((R, 1), F32), mask)
        carry_ref[...] = carry
        acc_ref[...] = weighted_values(a, new_rows(vn_ref))

    def page_rows(refs):
        return lambda h: jnp.concatenate([r[pl.ds(h, PAGE, stride=H), :] for r in refs], axis=0).astype(BF16)

    z = logits(page_rows(k_refs))
    sp = jnp.maximum(z, 0.0) + jnp.log1p(jnp.exp(-jnp.abs(z)))
    lk = -sp
    lk_st = jnp.concatenate([lk[:, g * PAGE:(g + 1) * PAGE] for g in range(G)], axis=0)
    after_st = _split_dot(lk_st, u)
    tot_st = after_st[:, 0:1] + lk_st[:, 0:1]
    carry = carry_ref[...]
    after = []
    for g in range(G):
        after.append(after_st[g * R:(g + 1) * R] + carry)
        carry = carry + tot_st[g * R:(g + 1) * R]
    carry_ref[...] = carry
    a = jnp.exp((z - sp) + jnp.concatenate(after, axis=1))
    acc_ref[...] += weighted_values(a, page_rows(v_refs))

    @pl.when(s_id == pl.num_programs(1) - 1)
    def _():
        o_ref[...] = _rmsnorm(acc_ref[...], gn_ref[...])


def _sb_sample(q, k_new, v_new, cache_k, cache_v, layer, page_table, bias_col, gn_rows, B, T):
    n_pages = page_table.shape[1]
    G = 8 if n_pages % 8 == 0 else 1
    nsteps = n_pages // G
    H, d = SB_HEADS, SB_DIM
    R = H * T

    def page_spec(g):
        return pl.BlockSpec((None, None, PAGE * H, d),
                            lambda b, s, pt: (layer, pt[b, n_pages - 1 - (s * G + g)], 0, 0))

    row_spec = pl.BlockSpec((None, R, d), lambda b, s, pt: (b, 0, 0))
    grid_spec = pltpu.PrefetchScalarGridSpec(
        num_scalar_prefetch=1,
        grid=(B, nsteps),
        in_specs=[row_spec, row_spec, row_spec,
                  pl.BlockSpec((R, 1), lambda b, s, pt: (0, 0)),
                  pl.BlockSpec((R, d), lambda b, s, pt: (0, 0))]
                 + [page_spec(g) for g in range(G)] + [page_spec(g) for g in range(G)],
        out_specs=row_spec,
        scratch_shapes=[pltpu.VMEM((R, d), F32), pltpu.VMEM((R, 1), F32)],
    )
    return pl.pallas_call(
        functools.partial(_sb_sample_kernel, T=T, G=G),
        grid_spec=grid_spec,
        out_shape=jax.ShapeDtypeStruct((B, R, d), F32),
        compiler_params=_params(("parallel", "arbitrary")),
        name="sb_sample",
    )(page_table, q, k_new, v_new, bias_col, gn_rows, *([cache_k] * G), *([cache_v] * G))


def _out_proj_kernel(a1_ref, a2_ref, w1_ref, w2_ref, x_ref, o_ref):
    o_ref[...] = (x_ref[...]
                  + jnp.dot(a1_ref[...].astype(BF16), w1_ref[...], preferred_element_type=F32)
                  + jnp.dot(a2_ref[...].astype(BF16), w2_ref[...], preferred_element_type=F32))


def _out_proj(a1, a2, w, x, tm):
    M, D = x.shape
    K1, K2 = a1.shape[1], a2.shape[1]
    return pl.pallas_call(
        _out_proj_kernel,
        grid=(M // tm,),
        in_specs=[pl.BlockSpec((tm, K1), lambda i: (i, 0)),
                  pl.BlockSpec((tm, K2), lambda i: (i, 0)),
                  pl.BlockSpec((K1, D), lambda i: (0, 0)),
                  pl.BlockSpec((K2, D), lambda i: (K1 // K2, 0)),
                  pl.BlockSpec((tm, D), lambda i: (i, 0))],
        out_specs=pl.BlockSpec((tm, D), lambda i: (i, 0)),
        out_shape=jax.ShapeDtypeStruct((M, D), F32),
        compiler_params=_params(("parallel",)),
        name="out_proj",
    )(a1, a2, w, w, x)


def _cross_kernel(x_ref, g_ref, wq_ref, mk_ref, mv_ref, wo_ref, o_ref):
    x = x_ref[...]
    hn = _rmsnorm(x, g_ref[...]).astype(BF16)
    q = jnp.dot(hn, wq_ref[...], preferred_element_type=F32)
    scale = X_DIM ** -0.5
    outs = []
    for h in range(X_HEADS):
        sl = slice(h * X_DIM, (h + 1) * X_DIM)
        s = lax.dot_general(q[:, sl].astype(BF16), mk_ref[:, sl].astype(BF16), NT_DIMS,
                            preferred_element_type=F32) * scale
        e = jnp.exp(s - jnp.max(s, axis=-1, keepdims=True))
        p = e / jnp.sum(e, axis=-1, keepdims=True)
        outs.append(jnp.dot(p.astype(BF16), mv_ref[:, sl].astype(BF16), preferred_element_type=F32))
    o = jnp.concatenate(outs, axis=1).astype(BF16)
    o_ref[...] = x + jnp.dot(o, wo_ref[...], preferred_element_type=F32)


def _cross(x, g, wq, wo, mk, mv, mk_spec, mv_spec, tm):
    M, D = x.shape
    return pl.pallas_call(
        _cross_kernel,
        grid=(M // tm,),
        in_specs=[pl.BlockSpec((tm, D), lambda i: (i, 0)),
                  pl.BlockSpec((1, D), lambda i: (0, 0)),
                  pl.BlockSpec((D, X_WIDTH), lambda i: (0, 0)),
                  mk_spec, mv_spec,
                  pl.BlockSpec((X_WIDTH, D), lambda i: (0, 0))],
        out_specs=pl.BlockSpec((tm, D), lambda i: (i, 0)),
        out_shape=jax.ShapeDtypeStruct((M, D), F32),
        compiler_params=_params(("parallel",)),
        name="cross",
    )(x, g.reshape(1, D), wq, mk, mv, wo)


def _ffn_kernel(*refs, tm, from_buf, tiles_per_seq):
    if from_buf:
        x_ref, g_ref, wa_ref, wb_ref, cw_ref, cb_ref, wd_ref, buf_ref, o_ref, st_ref, hn_ref, acc_ref = refs
    else:
        x_ref, xh_ref, g_ref, wa_ref, wb_ref, cw_ref, cb_ref, wd_ref, o_ref, st_ref, hn_ref, acc_ref = refs
    HALO = SUBLANES
    f = pl.program_id(1)

    @pl.when(f == 0)
    def _():
        hn_ref[HALO:, :] = _rmsnorm(x_ref[...], g_ref[...]).astype(BF16)
        if not from_buf:
            hn_ref[:HALO, :] = _rmsnorm(xh_ref[...], g_ref[...]).astype(BF16)
        acc_ref[...] = jnp.zeros(acc_ref.shape, F32)

    tf = wa_ref.shape[1]
    if from_buf:
        a = jnp.dot(hn_ref[HALO:, :], wa_ref[...], preferred_element_type=F32)
        a_ext = jnp.concatenate([jnp.zeros((HALO - (CONV_W - 1), tf), F32), buf_ref[...], a], axis=0)
    else:
        a_ext = jnp.dot(hn_ref[...], wa_ref[...], preferred_element_type=F32)
        seq_start = (pl.program_id(0) % tiles_per_seq) == 0
        rows = lax.broadcasted_iota(jnp.int32, (HALO + tm, 1), 0)
        a_ext = jnp.where(jnp.logical_and(seq_start, rows < HALO), 0.0, a_ext)
    a0 = a_ext[HALO:, :]
    a1 = pltpu.roll(a_ext, 1, axis=0)[HALO:, :]
    a2 = pltpu.roll(a_ext, 2, axis=0)[HALO:, :]
    b = jnp.dot(hn_ref[HALO:, :], wb_ref[...], preferred_element_type=F32)
    c = cb_ref[...] + ((a2 * cw_ref[0:1, :] + a1 * cw_ref[1:2, :]) + a0 * cw_ref[2:3, :])
    gate = (c * jax.nn.sigmoid(c)) * b
    acc_ref[...] += jnp.dot(gate.astype(BF16), wd_ref[...], preferred_element_type=F32)
    st_ref[...] = a0[tm - SUBLANES:, :]

    @pl.when(f == pl.num_programs(1) - 1)
    def _():
        o_ref[...] = x_ref[...] + acc_ref[...]


def _ffn(x, g, w_up, conv_w, conv_b, w_down, B, T, tm, tf, buf=None, layer=None):
    M, D = x.shape
    F = w_down.shape[0]
    nf = F // tf
    assert F % tf == 0 and T % tm == 0 and tm % SUBLANES == 0
    tiles_per_seq = T // tm
    from_buf = buf is not None
    hb = tm // SUBLANES
    in_specs = [pl.BlockSpec((tm, D), lambda i, f: (i, 0))]
    args = [x]
    if not from_buf:
        in_specs.append(pl.BlockSpec((SUBLANES, D), lambda i, f: (jnp.maximum(i * hb - 1, 0), 0)))
        args.append(x)
    in_specs += [pl.BlockSpec((1, D), lambda i, f: (0, 0)),
                 pl.BlockSpec((D, tf), lambda i, f: (0, f)),
                 pl.BlockSpec((D, tf), lambda i, f: (0, nf + f)),
                 pl.BlockSpec((CONV_W, tf), lambda i, f: (0, f)),
                 pl.BlockSpec((1, tf), lambda i, f: (0, f)),
                 pl.BlockSpec((tf, D), lambda i, f: (f, 0))]
    args += [g.reshape(1, D), w_up, w_up, conv_w, conv_b.reshape(1, F), w_down]
    if from_buf:
        assert tiles_per_seq == 1
        in_specs.append(pl.BlockSpec((None, None, CONV_W - 1, tf), lambda i, f: (layer, i, 0, f)))
        args.append(buf)
    y, st = pl.pallas_call(
        functools.partial(_ffn_kernel, tm=tm, from_buf=from_buf, tiles_per_seq=tiles_per_seq),
        grid=(M // tm, nf),
        in_specs=in_specs,
        out_specs=[pl.BlockSpec((tm, D), lambda i, f: (i, 0)),
                   pl.BlockSpec((None, SUBLANES, tf), lambda i, f: (i, 0, f))],
        out_shape=[jax.ShapeDtypeStruct((M, D), F32),
                   jax.ShapeDtypeStruct((M // tm, SUBLANES, F), F32)],
        scratch_shapes=[pltpu.VMEM((SUBLANES + tm, D), BF16), pltpu.VMEM((tm, D), F32)],
        compiler_params=_params(("parallel", "arbitrary")),
        name="ffn",
    )(*args)
    return y, st[tiles_per_seq - 1::tiles_per_seq, SUBLANES - (CONV_W - 1):, :]


def _norm_kernel(x_ref, g_ref, o_ref):
    o_ref[...] = _rmsnorm(x_ref[...], g_ref[...])


def _final_norm(x, g, tm):
    M, D = x.shape
    return pl.pallas_call(
        _norm_kernel,
        grid=(M // tm,),
        in_specs=[pl.BlockSpec((tm, D), lambda i: (i, 0)), pl.BlockSpec((1, D), lambda i: (0, 0))],
        out_specs=pl.BlockSpec((tm, D), lambda i: (i, 0)),
        out_shape=jax.ShapeDtypeStruct((M, D), F32),
        compiler_params=_params(("parallel",)),
        name="final_norm",
    )(x, g.reshape(1, D))


def _prep_layer_weights(w_in, b_gates, w_out, wx_q, wx_kv, wx_o, w_up, w_down):
    D = w_in.shape[0]
    g0 = 4 * ML_WIDTH
    g1 = g0 + 2 * ML_HEADS
    w_main = jnp.concatenate([w_in[:, :g0], w_in[:, g1:]], axis=1).astype(BF16)
    w_gate = jnp.concatenate([w_in[:, g0:g1], jnp.zeros((D, LANES - 2 * ML_HEADS), F32)], axis=1).astype(BF16)
    bg = jnp.concatenate([b_gates, jnp.zeros((LANES - 2 * ML_HEADS,), F32)]).reshape(1, LANES)
    return dict(w_main=w_main, w_gate=w_gate, bg=bg, w_out=w_out.astype(BF16), wx_q=wx_q.astype(BF16),
                wx_kv=wx_kv.astype(BF16), wx_o=wx_o.astype(BF16), w_up=w_up.astype(BF16),
                w_down=w_down.astype(BF16))


def _mix_block(x, B, T, w, norm_mix, tm, tn):
    proj = _norm_matmul(x, norm_mix, w["w_main"], tm, tn)
    gates = _norm_matmul(x, norm_mix, w["w_gate"], tm, LANES)
    return proj, gates


def _prompt_layer(x, mem, B, T, w, norm_mix, b_sb, g_heads, norm_cross, norm_mem, norm_ffn, conv_w, conv_b):
    n_mem = mem.shape[0] // B
    kv = _norm_matmul(mem, norm_mem, w["wx_kv"], min(512, mem.shape[0]), 512)
    tm = min(512, T)
    proj, gates = _mix_block(x, B, T, w, norm_mix, tm, 1024)
    L = min(LANES, T)
    mix_ml, c, n, m = _mlstm(proj, gates, w["bg"], g_heads[:ML_WIDTH], B, T, L, None, BF16)
    mix_sb = _sb_prompt(proj, b_sb, g_heads[ML_WIDTH:], B, T)
    x = _out_proj(mix_ml, mix_sb, w["w_out"], x, tm)
    tpb = T // tm
    x = _cross(x, norm_cross, w["wx_q"], w["wx_o"], kv, kv,
               pl.BlockSpec((n_mem, X_WIDTH), lambda i: (i // tpb, 0)),
               pl.BlockSpec((n_mem, X_WIDTH), lambda i: (i // tpb, 1)), tm)
    x, st = _ffn(x, norm_ffn, w["w_up"], conv_w, conv_b, w["w_down"], B, T, tm, 512)
    sbk = proj[:, COL_SK * SB_WIDTH:(COL_SK + 1) * SB_WIDTH].reshape(B, T, SB_HEADS, SB_DIM)
    sbv = proj[:, COL_SV * SB_WIDTH:(COL_SV + 1) * SB_WIDTH].reshape(B, T, SB_HEADS, SB_DIM)
    mk = kv[:, :X_WIDTH].reshape(B, n_mem, X_HEADS, X_DIM)
    mv = kv[:, X_WIDTH:].reshape(B, n_mem, X_HEADS, X_DIM)
    return x, (sbk, sbv, c, n, m, st, mk, mv)


def _sample_layer(x, B, T, layer, w, norm_mix, b_sb, g_heads, norm_cross, norm_ffn, conv_w, conv_b,
                  cache_k, cache_v, page_table, c0, n0, m0, conv_state, mem_k, mem_v):
    M = B * T
    proj, gates = _mix_block(x, B, T, w, norm_mix, M, 1024)
    mix_ml, c, n, m = _mlstm(proj, gates, w["bg"], g_heads[:ML_WIDTH], B, T, T, (c0, n0, m0), F32)
    H, d = SB_HEADS, SB_DIM
    heads = lambda colblk: proj[:, colblk * SB_WIDTH:(colblk + 1) * SB_WIDTH].reshape(B, T, H, d)
    sbk, sbv = heads(COL_SK), heads(COL_SV)
    head_rows = lambda a: a.transpose(0, 2, 1, 3).reshape(B, H * T, d)
    bias_col = jnp.repeat(b_sb, T).reshape(H * T, 1)
    gn_rows = jnp.repeat(g_heads[ML_WIDTH:].reshape(H, d), T, axis=0)
    mix_sb = _sb_sample(head_rows(heads(COL_SQ)), head_rows(sbk), head_rows(sbv), cache_k, cache_v, layer,
                        page_table, bias_col, gn_rows, B, T)
    mix_sb = mix_sb.reshape(B, H, T, d).transpose(0, 2, 1, 3).reshape(M, SB_WIDTH)
    x = _out_proj(mix_ml, mix_sb, w["w_out"], x, M)
    n_mem = mem_k.shape[2]
    mem_spec = pl.BlockSpec((None, None, n_mem, X_WIDTH), lambda i: (layer, i, 0, 0))
    x = _cross(x, norm_cross, w["wx_q"], w["wx_o"], mem_k, mem_v, mem_spec, mem_spec, T)
    x, st = _ffn(x, norm_ffn, w["w_up"], conv_w, conv_b, w["w_down"], B, T, T, 512, buf=conv_state, layer=layer)
    return x, (sbk, sbv, c, n, m, st)


def kernel(x_prompt, x_sample, mem_prompt, cache_sb_k, cache_sb_v, state_mlstm_c, state_mlstm_n, state_mlstm_m,
           state_ffn_conv, cache_mem_k, cache_mem_v, page_table, norm_mix, w_in, b_gates, b_sb, norm_heads, w_out,
           norm_cross, norm_mem, wx_q, wx_kv, wx_o, norm_ffn, w_up, conv_w, conv_b, w_down, norm_final):
    Bp, Tp, D = x_prompt.shape
    Bs, Ts, _ = x_sample.shape
    depth = w_in.shape[0]
    n_mem = cache_mem_k.shape[2]
    n_pool = cache_sb_k.shape[1]
    cache_k = cache_sb_k.reshape(depth, n_pool, PAGE * SB_HEADS, SB_DIM)
    cache_v = cache_sb_v.reshape(depth, n_pool, PAGE * SB_HEADS, SB_DIM)
    mem_k = cache_mem_k.reshape(depth, Bs, n_mem, X_WIDTH)
    mem_v = cache_mem_v.reshape(depth, Bs, n_mem, X_WIDTH)
    yp = x_prompt.reshape(Bp * Tp, D)
    ys = x_sample.reshape(Bs * Ts, D)
    mem = mem_prompt.reshape(Bp * mem_prompt.shape[1], D)
    P = [[] for _ in range(8)]
    S = [[] for _ in range(6)]
    for l in range(depth):
        w = _prep_layer_weights(w_in[l], b_gates[l], w_out[l], wx_q[l], wx_kv[l], wx_o[l], w_up[l], w_down[l])
        yp, st_p = _prompt_layer(yp, mem, Bp, Tp, w, norm_mix[l], b_sb[l], norm_heads[l], norm_cross[l],
                                 norm_mem[l], norm_ffn[l], conv_w[l], conv_b[l])
        for lst, a in zip(P, st_p):
            lst.append(a)
        ys, st_s = _sample_layer(ys, Bs, Ts, l, w, norm_mix[l], b_sb[l], norm_heads[l], norm_cross[l],
                                 norm_ffn[l], conv_w[l], conv_b[l], cache_k, cache_v, page_table,
                                 state_mlstm_c[l], state_mlstm_n[l], state_mlstm_m[l], state_ffn_conv,
                                 mem_k, mem_v)
        for lst, a in zip(S, st_s):
            lst.append(a)
    y_prompt = _final_norm(yp, norm_final, min(512, Bp * Tp)).reshape(Bp, Tp, D)
    y_sample = _final_norm(ys, norm_final, Bs * Ts).reshape(Bs, Ts, D)
    return (y_prompt, y_sample) + tuple(jnp.stack(a) for a in P) + tuple(jnp.stack(a) for a in S)
```

```python
import functools

import jax
import jax.numpy as jnp
from jax import lax
from jax.experimental import pallas as pl
from jax.experimental.pallas import tpu as pltpu

F32 = jnp.float32
BF16 = jnp.bfloat16

EPS = 1e-6
M_EMPTY = -1e30
ML_HEADS = 4
ML_DIM = 256
SB_HEADS = 8
SB_DIM = 128
X_HEADS = 4
X_DIM = 128
PAGE = 128
CONV_W = 3
ML_WIDTH = ML_HEADS * ML_DIM
SB_WIDTH = SB_HEADS * SB_DIM
X_WIDTH = X_HEADS * X_DIM

SUBLANES = 8
LANES = 128
MIB = 1024 * 1024
VMEM_LIMIT = 56 * MIB

COL_MQ, COL_MK, COL_MV, COL_MO = 0, 1, 2, 3
COL_SQ, COL_SK, COL_SV = 4, 5, 6
N_PROJ = 7 * ML_WIDTH

NT_DIMS = (((1,), (1,)), ((), ()))
TN_DIMS = (((0,), (0,)), ((), ()))


def _params(sem):
    return pltpu.CompilerParams(dimension_semantics=sem, vmem_limit_bytes=VMEM_LIMIT)


def _rmsnorm(xf, g):
    return xf * lax.rsqrt(jnp.mean(xf * xf, axis=-1, keepdims=True) + EPS) * g


def _log_sigmoid(x):
    return jnp.minimum(x, 0.0) - jnp.log1p(jnp.exp(-jnp.abs(x)))


def _softplus(z):
    t = jnp.exp(-jnp.abs(z))
    u = 1.0 + t
    return jnp.maximum(z, 0.0) + (jnp.log(u) + (t - (u - 1.0)))


def _split_dot(x, u):
    hi = x.astype(BF16)
    lo = (x - hi.astype(F32)).astype(BF16)
    return (jnp.dot(hi, u, preferred_element_type=F32) + jnp.dot(lo, u, preferred_element_type=F32))


def _norm_matmul_kernel(x_ref, g_ref, w_ref, o_ref, hn_ref):
    @pl.when(pl.program_id(1) == 0)
    def _():
        hn_ref[...] = _rmsnorm(x_ref[...], g_ref[...]).astype(BF16)

    o_ref[...] = jnp.dot(hn_ref[...], w_ref[...], preferred_element_type=F32)


def _norm_matmul(x, g, w, tm, tn):
    M, D = x.shape
    N = w.shape[1]
    assert M % tm == 0 and N % tn == 0
    return pl.pallas_call(
        _norm_matmul_kernel,
        grid=(M // tm, N // tn),
        in_specs=[pl.BlockSpec((tm, D), lambda i, j: (i, 0)),
                  pl.BlockSpec((1, D), lambda i, j: (0, 0)),
                  pl.BlockSpec((D, tn), lambda i, j: (0, j))],
        out_specs=pl.BlockSpec((tm, tn), lambda i, j: (i, j)),
        out_shape=jax.ShapeDtypeStruct((M, N), F32),
        scratch_shapes=[pltpu.VMEM((tm, D), BF16)],
        compiler_params=_params(("parallel", "arbitrary")),
        name="norm_matmul",
    )(x, g.reshape(1, D), w)


def _mlstm_kernel(q_ref, k_ref, v_ref, o_ref, gt_ref, bg_ref, gn_ref, *rest, L, has_init):
    if has_init:
        c0_ref, n0_ref, m0_ref, mix_ref, c_ref, n_ref, m_ref = rest
    else:
        mix_ref, c_ref, n_ref, m_ref = rest
    S = LANES
    H, d = ML_HEADS, ML_DIM

    @pl.when(pl.program_id(1) == 0)
    def _():
        if has_init:
            c_ref[...] = c0_ref[...]
            n_ref[...] = n0_ref[...]
            m_ref[...] = m0_ref[...]
        else:
            c_ref[...] = jnp.zeros(c_ref.shape, F32)
            n_ref[...] = jnp.zeros(n_ref.shape, F32)
            m_ref[...] = jnp.full(m_ref.shape, M_EMPTY, F32)

    def pad(a):
        if L == S:
            return a
        return jnp.concatenate([a, jnp.zeros((S - L, a.shape[1]), a.dtype)], axis=0)

    gates = pad(gt_ref[...] + bg_ref[...])
    row = lax.broadcasted_iota(jnp.int32, (S, S), 0)
    col = lax.broadcasted_iota(jnp.int32, (S, S), 1)
    causal = col <= row
    valid = lax.broadcasted_iota(jnp.int32, (S, 1), 0) < L
    b_all = jnp.dot(causal.astype(F32), _log_sigmoid(gates), precision=lax.Precision.HIGHEST,
                    preferred_element_type=F32)
    r_t = (gates - pltpu.roll(b_all, LANES - H, axis=1)).T

    for h in range(H):
        hs = slice(h * d, (h + 1) * d)
        i_col = gates[:, h:h + 1]
        b_col = b_all[:, H + h:H + h + 1]
        g_col = b_col + m_ref[h]
        logd = b_col + r_t[h:h + 1, :]
        mt = jnp.maximum(g_col, jnp.max(jnp.where(causal, logd, -jnp.inf), axis=-1, keepdims=True))

        q = pad(q_ref[:, hs])
        k = pad(k_ref[:, hs]) * (d ** -0.5)
        v = pad(v_ref[:, hs])
        qb, kb = q.astype(BF16), k.astype(BF16)
        s = lax.dot_general(qb, kb, NT_DIMS, preferred_element_type=F32)
        s = jnp.where(causal, s * jnp.exp(logd - mt), 0.0)
        gw = jnp.exp(g_col - mt)
        C = c_ref[0, h]
        n_row = n_ref[h]
        num = (jnp.dot(s.astype(BF16), v.astype(BF16), preferred_element_type=F32)
               + gw * jnp.dot(qb, C.astype(BF16), preferred_element_type=F32))
        den = jnp.sum(s, axis=-1, keepdims=True) + gw * jnp.sum(q * n_row, axis=-1, keepdims=True)
        den = jnp.maximum(jnp.abs(den), jnp.exp(-mt))
        hcell = num / den

        m_new = mt[L - 1:L, :]
        wk = jnp.where(valid, jnp.exp(b_col[L - 1:L, :] - b_col + i_col - m_new), 0.0)
        decay = jnp.exp(g_col[L - 1:L, :] - m_new)
        c_ref[0, h] = decay * C + lax.dot_general(kb, (wk * v).astype(BF16), TN_DIMS,
                                                  preferred_element_type=F32)
        n_ref[h] = decay * n_row + jnp.sum(wk * k, axis=0, keepdims=True)
        m_ref[h] = m_new

        hm = jax.nn.sigmoid(o_ref[:, hs]) * hcell[:L]
        mix_ref[:, hs] = _rmsnorm(hm, gn_ref[:, hs]).astype(mix_ref.dtype)


def _mlstm(proj, gates, b_gates, g_ml, B, T, L, init, mix_dtype):
    nc = T // L
    H, d = ML_HEADS, ML_DIM
    W = ML_WIDTH
    has_init = init is not None

    def qkvo(colblk):
        return pl.BlockSpec((L, W), lambda b, c: (b * nc + c, colblk))

    state_specs = [pl.BlockSpec((1, H, d, d), lambda b, c: (b, 0, 0, 0)),
                   pl.BlockSpec((H, 1, d), lambda b, c: (b, 0, 0)),
                   pl.BlockSpec((H, 1, 1), lambda b, c: (b, 0, 0))]
    in_specs = [qkvo(COL_MQ), qkvo(COL_MK), qkvo(COL_MV), qkvo(COL_MO),
                pl.BlockSpec((L, LANES), lambda b, c: (b * nc + c, 0)),
                pl.BlockSpec((1, LANES), lambda b, c: (0, 0)),
                pl.BlockSpec((1, W), lambda b, c: (0, 0))]
    args = [proj, proj, proj, proj, gates, b_gates, g_ml.reshape(1, W)]
    if has_init:
        c0, n0, m0 = init
        in_specs += state_specs
        args += [c0, n0.reshape(B * H, 1, d), m0.reshape(B * H, 1, 1)]
    mix, c, n, m = pl.pallas_call(
        functools.partial(_mlstm_kernel, L=L, has_init=has_init),
        grid=(B, nc),
        in_specs=in_specs,
        out_specs=[pl.BlockSpec((L, W), lambda b, c: (b * nc + c, 0))] + state_specs,
        out_shape=[jax.ShapeDtypeStruct((B * T, W), mix_dtype),
                   jax.ShapeDtypeStruct((B, H, d, d), F32),
                   jax.ShapeDtypeStruct((B * H, 1, d), F32),
                   jax.ShapeDtypeStruct((B * H, 1, 1), F32)],
        compiler_params=_params(("parallel", "arbitrary")),
        name="mlstm",
    )(*args)
    return mix, c, n.reshape(B, H, d), m.reshape(B, H)


def _sb_block(z, u, carry, mask):
    sp = _softplus(z)
    lk = -sp
    if mask is not None:
        lk = jnp.where(mask, lk, 0.0)
    after = _split_dot(lk, u)
    a = jnp.exp((z - sp) + after + carry)
    if mask is not None:
        a = jnp.where(mask, a, 0.0)
    return a, carry + (after[:, 0:1] + lk[:, 0:1])


def _sb_prompt_kernel(bias_ref, q_ref, k_ref, v_ref, gn_ref, o_ref, kb_ref, vb_ref, ls_ref, lk_ref, e_ref, *, TQ):
    h = pl.program_id(1)
    i = pl.program_id(2)

    @pl.when(i == 0)
    def _():
        kb_ref[...] = k_ref[...].astype(BF16)
        vb_ref[...] = v_ref[...].astype(BF16)

    bias = bias_ref[h]
    scale = SB_DIM ** -0.5
    qb = q_ref[...].astype(BF16)
    row = lax.broadcasted_iota(jnp.int32, (TQ, TQ), 0)
    col = lax.broadcasted_iota(jnp.int32, (TQ, TQ), 1)
    u = (row > col).astype(BF16)

    def rows(j):
        return pl.ds(pl.multiple_of(jnp.maximum(j, 0) * TQ, TQ), TQ)

    def logits(j):
        return lax.dot_general(qb, kb_ref[rows(j), :], NT_DIMS, preferred_element_type=F32) * scale + bias

    def store_log_gates(z, mask):
        sp = _softplus(z)
        ls_ref[...] = z - sp
        lk_ref[...] = -sp if mask is None else jnp.where(mask, -sp, 0.0)

    def exponent(mask):
        lk = lk_ref[...]
        after = _split_dot(lk, u)
        e = ls_ref[...] + after
        if mask is not None:
            e = jnp.where(mask, e, M_EMPTY)
        return e, after[:, 0:1] + lk[:, 0:1]

    store_log_gates(logits(i), col < row)
    z1 = logits(i - 1)
    e0, tot0 = exponent(col < row)
    e_ref[...] = e0
    store_log_gates(z1, None)

    def body(t, st):
        tot, carry, acc = st
        j = i - t
        z = logits(j - 2)
        a = jnp.exp(e_ref[...] + carry)
        acc = acc + jnp.dot(a.astype(BF16), vb_ref[rows(j), :], preferred_element_type=F32)
        e_new, tot_new = exponent(None)
        e_ref[...] = e_new
        store_log_gates(z, None)
        return tot_new, carry + tot, acc

    _, _, acc = lax.fori_loop(0, i + 1, body, (tot0, jnp.zeros((TQ, 1), F32), jnp.zeros((TQ, SB_DIM), F32)))
    o_ref[...] = _rmsnorm(acc, gn_ref[...]).astype(o_ref.dtype)


def _sb_prompt(proj, b_sb, g_sb, B, T):
    TQ = min(256, T)
    nq = T // TQ
    H, d = SB_HEADS, SB_DIM
    return pl.pallas_call(
        functools.partial(_sb_prompt_kernel, TQ=TQ),
        grid=(B, H, nq),
        in_specs=[pl.BlockSpec(memory_space=pltpu.SMEM),
                  pl.BlockSpec((TQ, d), lambda b, h, i: (b * nq + i, COL_SQ * H + h)),
                  pl.BlockSpec((T, d), lambda b, h, i: (b, COL_SK * H + h)),
                  pl.BlockSpec((T, d), lambda b, h, i: (b, COL_SV * H + h)),
                  pl.BlockSpec((1, d), lambda b, h, i: (0, h))],
        out_specs=pl.BlockSpec((TQ, d), lambda b, h, i: (b * nq + i, h)),
        out_shape=jax.ShapeDtypeStruct((B * T, SB_WIDTH), BF16),
        scratch_shapes=[pltpu.VMEM((T, d), BF16), pltpu.VMEM((T, d), BF16),
                        pltpu.VMEM((TQ, TQ), F32), pltpu.VMEM((TQ, TQ), F32), pltpu.VMEM((TQ, TQ), F32)],
        compiler_params=_params(("parallel", "parallel", "arbitrary")),
        name="sb_prompt",
    )(b_sb, proj, proj, proj, g_sb.reshape(1, SB_WIDTH))


def _sb_sample_kernel(pt_ref, q_ref, kn_ref, vn_ref, bias_ref, gn_ref, *rest, T, G):
    k_refs, v_refs = rest[:G], rest[G:2 * G]
    o_ref, acc_ref, carry_ref = rest[2 * G:]
    H, d = SB_HEADS, SB_DIM
    R = H * T
    s_id = pl.program_id(1)
    scale = d ** -0.5
    row = lax.broadcasted_iota(jnp.int32, (PAGE, PAGE), 0)
    col = lax.broadcasted_iota(jnp.int32, (PAGE, PAGE), 1)
    u = (row > col).astype(BF16)
    qb = q_ref[...].astype(BF16)
    bias = bias_ref[...]

    def logits(keys_of_head):
        z = [lax.dot_general(qb[h * T:(h + 1) * T], keys_of_head(h), NT_DIMS, preferred_element_type=F32)
             for h in range(H)]
        return jnp.concatenate(z, axis=0) * scale + bias

    def weighted_values(a, values_of_head):
        ab = a.astype(BF16)
        o = [jnp.dot(ab[h * T:(h + 1) * T], values_of_head(h), preferred_element_type=F32) for h in range(H)]
        return jnp.concatenate(o, axis=0)

    @pl.when(s_id == 0)
    def _():
        zpad = jnp.zeros((PAGE - T, d), F32)
        new_rows = lambda ref: lambda h: jnp.concatenate([ref[h * T:(h + 1) * T, :], zpad], axis=0).astype(BF16)
        z = logits(new_rows(kn_ref))
        mask = (lax.broadcasted_iota(jnp.int32, (R, PAGE), 1) < lax.broadcasted_iota(jnp.int32, (R, PAGE), 0) % T)
        a, carry = _sb_block(z, u, jnp.zeros((R, 1), F32), mask)
        carry_ref[...] = carry
        acc_ref[...] = weighted_values(a, new_rows(vn_ref))

    def page_rows(refs):
        return lambda h: jnp.concatenate([r[pl.ds(h, PAGE, stride=H), :] for r in refs], axis=0).astype(BF16)

    z = logits(page_rows(k_refs))
    sp = _softplus(z)
    lk = -sp
    lk_st = jnp.concatenate([lk[:, g * PAGE:(g + 1) * PAGE] for g in range(G)], axis=0)
    after_st = _split_dot(lk_st, u)
    tot_st = after_st[:, 0:1] + lk_st[:, 0:1]
    carry = carry_ref[...]
    after = []
    for g in range(G):
        after.append(after_st[g * R:(g + 1) * R] + carry)
        carry = carry + tot_st[g * R:(g + 1) * R]
    carry_ref[...] = carry
    a = jnp.exp((z - sp) + jnp.concatenate(after, axis=1))
    acc_ref[...] += weighted_values(a, page_rows(v_refs))

    @pl.when(s_id == pl.num_programs(1) - 1)
    def _():
        o_ref[...] = _rmsnorm(acc_ref[...], gn_ref[...])


def _sb_sample(q, k_new, v_new, cache_k, cache_v, layer, page_table, bias_col, gn_rows, B, T):
    n_pages = page_table.shape[1]
    G = 8 if n_pages % 8 == 0 else 1
    nsteps = n_pages // G
    H, d = SB_HEADS, SB_DIM
    R = H * T

    def page_spec(g):
        return pl.BlockSpec((None, None, PAGE * H, d),
                            lambda b, s, pt: (layer, pt[b, n_pages - 1 - (s * G + g)], 0, 0))

    row_spec = pl.BlockSpec((None, R, d), lambda b, s, pt: (b, 0, 0))
    grid_spec = pltpu.PrefetchScalarGridSpec(
        num_scalar_prefetch=1,
        grid=(B, nsteps),
        in_specs=[row_spec, row_spec, row_spec,
                  pl.BlockSpec((R, 1), lambda b, s, pt: (0, 0)),
                  pl.BlockSpec((R, d), lambda b, s, pt: (0, 0))]
                 + [page_spec(g) for g in range(G)] + [page_spec(g) for g in range(G)],
        out_specs=row_spec,
        scratch_shapes=[pltpu.VMEM((R, d), F32), pltpu.VMEM((R, 1), F32)],
    )
    return pl.pallas_call(
        functools.partial(_sb_sample_kernel, T=T, G=G),
        grid_spec=grid_spec,
        out_shape=jax.ShapeDtypeStruct((B, R, d), F32),
        compiler_params=_params(("parallel", "arbitrary")),
        name="sb_sample",
    )(page_table, q, k_new, v_new, bias_col, gn_rows, *([cache_k] * G), *([cache_v] * G))


def _out_proj_kernel(a1_ref, a2_ref, w1_ref, w2_ref, x_ref, o_ref):
    o_ref[...] = (x_ref[...]
                  + jnp.dot(a1_ref[...].astype(BF16), w1_ref[...], preferred_element_type=F32)
                  + jnp.dot(a2_ref[...].astype(BF16), w2_ref[...], preferred_element_type=F32))


def _out_proj(a1, a2, w, x, tm):
    M, D = x.shape
    K1, K2 = a1.shape[1], a2.shape[1]
    return pl.pallas_call(
        _out_proj_kernel,
        grid=(M // tm,),
        in_specs=[pl.BlockSpec((tm, K1), lambda i: (i, 0)),
                  pl.BlockSpec((tm, K2), lambda i: (i, 0)),
                  pl.BlockSpec((K1, D), lambda i: (0, 0)),
                  pl.BlockSpec((K2, D), lambda i: (K1 // K2, 0)),
                  pl.BlockSpec((tm, D), lambda i: (i, 0))],
        out_specs=pl.BlockSpec((tm, D), lambda i: (i, 0)),
        out_shape=jax.ShapeDtypeStruct((M, D), F32),
        compiler_params=_params(("parallel",)),
        name="out_proj",
    )(a1, a2, w, w, x)


def _cross_kernel(x_ref, g_ref, wq_ref, mk_ref, mv_ref, wo_ref, o_ref):
    x = x_ref[...]
    hn = _rmsnorm(x, g_ref[...]).astype(BF16)
    q = jnp.dot(hn, wq_ref[...], preferred_element_type=F32)
    scale = X_DIM ** -0.5
    outs = []
    for h in range(X_HEADS):
        sl = slice(h * X_DIM, (h + 1) * X_DIM)
        s = lax.dot_general(q[:, sl].astype(BF16), mk_ref[:, sl].astype(BF16), NT_DIMS,
                            preferred_element_type=F32) * scale
        e = jnp.exp(s - jnp.max(s, axis=-1, keepdims=True))
        p = e / jnp.sum(e, axis=-1, keepdims=True)
        outs.append(jnp.dot(p.astype(BF16), mv_ref[:, sl].astype(BF16), preferred_element_type=F32))
    o = jnp.concatenate(outs, axis=1).astype(BF16)
    o_ref[...] = x + jnp.dot(o, wo_ref[...], preferred_element_type=F32)


def _cross(x, g, wq, wo, mk, mv, mk_spec, mv_spec, tm):
    M, D = x.shape
    return pl.pallas_call(
        _cross_kernel,
        grid=(M // tm,),
        in_specs=[pl.BlockSpec((tm, D), lambda i: (i, 0)),
                  pl.BlockSpec((1, D), lambda i: (0, 0)),
                  pl.BlockSpec((D, X_WIDTH), lambda i: (0, 0)),
                  mk_spec, mv_spec,
                  pl.BlockSpec((X_WIDTH, D), lambda i: (0, 0))],
        out_specs=pl.BlockSpec((tm, D), lambda i: (i, 0)),
        out_shape=jax.ShapeDtypeStruct((M, D), F32),
        compiler_params=_params(("parallel",)),
        name="cross",
    )(x, g.reshape(1, D), wq, mk, mv, wo)


def _ffn_kernel(*refs, tm, from_buf, tiles_per_seq):
    if from_buf:
        x_ref, g_ref, wa_ref, wb_ref, cw_ref, cb_ref, wd_ref, buf_ref, o_ref, st_ref, hn_ref, acc_ref = refs
    else:
        x_ref, xh_ref, g_ref, wa_ref, wb_ref, cw_ref, cb_ref, wd_ref, o_ref, st_ref, hn_ref, acc_ref = refs
    HALO = SUBLANES
    f = pl.program_id(1)

    @pl.when(f == 0)
    def _():
        hn_ref[HALO:, :] = _rmsnorm(x_ref[...], g_ref[...]).astype(BF16)
        if not from_buf:
            hn_ref[:HALO, :] = _rmsnorm(xh_ref[...], g_ref[...]).astype(BF16)
        acc_ref[...] = jnp.zeros(acc_ref.shape, F32)

    tf = wa_ref.shape[1]
    if from_buf:
        a = jnp.dot(hn_ref[HALO:, :], wa_ref[...], preferred_element_type=F32)
        nb = buf_ref.shape[1]
        a_ext = jnp.concatenate([buf_ref[j] for j in range(CONV_W - 1)] + [a], axis=0)
        taps = [a_ext[j * nb:j * nb + tm, :] for j in range(CONV_W)]
    else:
        a_ext = jnp.dot(hn_ref[...], wa_ref[...], preferred_element_type=F32)
        seq_start = (pl.program_id(0) % tiles_per_seq) == 0
        rows = lax.broadcasted_iota(jnp.int32, (HALO + tm, 1), 0)
        a_ext = jnp.where(jnp.logical_and(seq_start, rows < HALO), 0.0, a_ext)
        taps = [pltpu.roll(a_ext, 2, axis=0)[HALO:, :], pltpu.roll(a_ext, 1, axis=0)[HALO:, :], a_ext[HALO:, :]]
    b = jnp.dot(hn_ref[HALO:, :], wb_ref[...], preferred_element_type=F32)
    c = cb_ref[...] + ((taps[0] * cw_ref[0:1, :] + taps[1] * cw_ref[1:2, :]) + taps[2] * cw_ref[2:3, :])
    gate = (c * jax.nn.sigmoid(c)) * b
    acc_ref[...] += jnp.dot(gate.astype(BF16), wd_ref[...], preferred_element_type=F32)
    st_ref[...] = taps[2][tm - st_ref.shape[0]:, :]

    @pl.when(f == pl.num_programs(1) - 1)
    def _():
        o_ref[...] = x_ref[...] + acc_ref[...]


def _ffn(x, g, w_up, conv_w, conv_b, w_down, B, T, tm, tf, buf=None, layer=None):
    M, D = x.shape
    F = w_down.shape[0]
    nf = F // tf
    from_buf = buf is not None
    assert F % tf == 0 and tm % SUBLANES == 0 and (tm == M and B % SUBLANES == 0 if from_buf else T % tm == 0)
    tiles_per_seq = 1 if from_buf else T // tm
    st_rows = (CONV_W - 1) * B if from_buf else SUBLANES
    hb = tm // SUBLANES
    in_specs = [pl.BlockSpec((tm, D), lambda i, f: (i, 0))]
    args = [x]
    if not from_buf:
        in_specs.append(pl.BlockSpec((SUBLANES, D), lambda i, f: (jnp.maximum(i * hb - 1, 0), 0)))
        args.append(x)
    in_specs += [pl.BlockSpec((1, D), lambda i, f: (0, 0)),
                 pl.BlockSpec((D, tf), lambda i, f: (0, f)),
                 pl.BlockSpec((D, tf), lambda i, f: (0, nf + f)),
                 pl.BlockSpec((CONV_W, tf), lambda i, f: (0, f)),
                 pl.BlockSpec((1, tf), lambda i, f: (0, f)),
                 pl.BlockSpec((tf, D), lambda i, f: (f, 0))]
    args += [g.reshape(1, D), w_up, w_up, conv_w, conv_b.reshape(1, F), w_down]
    if from_buf:
        in_specs.append(pl.BlockSpec((None, CONV_W - 1, B, tf), lambda i, f: (layer, 0, 0, f)))
        args.append(buf)
    y, st = pl.pallas_call(
        functools.partial(_ffn_kernel, tm=tm, from_buf=from_buf, tiles_per_seq=tiles_per_seq),
        grid=(M // tm, nf),
        in_specs=in_specs,
        out_specs=[pl.BlockSpec((tm, D), lambda i, f: (i, 0)),
                   pl.BlockSpec((None, st_rows, tf), lambda i, f: (i, 0, f))],
        out_shape=[jax.ShapeDtypeStruct((M, D), F32),
                   jax.ShapeDtypeStruct((M // tm, st_rows, F), F32)],
        scratch_shapes=[pltpu.VMEM((SUBLANES + tm, D), BF16), pltpu.VMEM((tm, D), F32)],
        compiler_params=_params(("parallel", "arbitrary")),
        name="ffn",
    )(*args)
    if from_buf:
        return y, st.reshape(CONV_W - 1, B, F).transpose(1, 0, 2)
    return y, st[tiles_per_seq - 1::tiles_per_seq, SUBLANES - (CONV_W - 1):, :]


def _norm_kernel(x_ref, g_ref, o_ref):
    o_ref[...] = _rmsnorm(x_ref[...], g_ref[...])


def _final_norm(x, g, tm):
    M, D = x.shape
    return pl.pallas_call(
        _norm_kernel,
        grid=(M // tm,),
        in_specs=[pl.BlockSpec((tm, D), lambda i: (i, 0)), pl.BlockSpec((1, D), lambda i: (0, 0))],
        out_specs=pl.BlockSpec((tm, D), lambda i: (i, 0)),
        out_shape=jax.ShapeDtypeStruct((M, D), F32),
        compiler_params=_params(("parallel",)),
        name="final_norm",
    )(x, g.reshape(1, D))


def _prep_layer_weights(w_in, b_gates, w_out, wx_q, wx_kv, wx_o, w_up, w_down):
    D = w_in.shape[0]
    g0 = 4 * ML_WIDTH
    g1 = g0 + 2 * ML_HEADS
    w_main = jnp.concatenate([w_in[:, :g0].astype(BF16), w_in[:, g1:].astype(BF16)], axis=1)
    w_gate = jnp.concatenate([w_in[:, g0:g1], jnp.zeros((D, LANES - 2 * ML_HEADS), F32)], axis=1).astype(BF16)
    bg = jnp.concatenate([b_gates, jnp.zeros((LANES - 2 * ML_HEADS,), F32)]).reshape(1, LANES)
    return dict(w_main=w_main, w_gate=w_gate, bg=bg, w_out=w_out.astype(BF16), wx_q=wx_q.astype(BF16),
                wx_kv=wx_kv.astype(BF16), wx_o=wx_o.astype(BF16), w_up=w_up.astype(BF16),
                w_down=w_down.astype(BF16))


def _mix_block(x, B, T, w, norm_mix, tm, tn):
    proj = _norm_matmul(x, norm_mix, w["w_main"], tm, tn)
    gates = _norm_matmul(x, norm_mix, w["w_gate"], tm, LANES)
    return proj, gates


def _prompt_layer(x, mem, B, T, w, norm_mix, b_sb, g_heads, norm_cross, norm_mem, norm_ffn, conv_w, conv_b):
    n_mem = mem.shape[0] // B
    kv = _norm_matmul(mem, norm_mem, w["wx_kv"], min(512, mem.shape[0]), 512)
    tm = min(512, T)
    proj, gates = _mix_block(x, B, T, w, norm_mix, min(1024, T), 1024)
    L = min(LANES, T)
    mix_ml, c, n, m = _mlstm(proj, gates, w["bg"], g_heads[:ML_WIDTH], B, T, L, None, BF16)
    mix_sb = _sb_prompt(proj, b_sb, g_heads[ML_WIDTH:], B, T)
    x = _out_proj(mix_ml, mix_sb, w["w_out"], x, tm)
    tpb = T // tm
    x = _cross(x, norm_cross, w["wx_q"], w["wx_o"], kv, kv,
               pl.BlockSpec((n_mem, X_WIDTH), lambda i: (i // tpb, 0)),
               pl.BlockSpec((n_mem, X_WIDTH), lambda i: (i // tpb, 1)), tm)
    x, st = _ffn(x, norm_ffn, w["w_up"], conv_w, conv_b, w["w_down"], B, T, tm, 512)
    sbk = proj[:, COL_SK * SB_WIDTH:(COL_SK + 1) * SB_WIDTH].reshape(B, T, SB_HEADS, SB_DIM)
    sbv = proj[:, COL_SV * SB_WIDTH:(COL_SV + 1) * SB_WIDTH].reshape(B, T, SB_HEADS, SB_DIM)
    mk = kv[:, :X_WIDTH].reshape(B, n_mem, X_HEADS, X_DIM)
    mv = kv[:, X_WIDTH:].reshape(B, n_mem, X_HEADS, X_DIM)
    return x, (sbk, sbv, c, n, m, st, mk, mv)


def _sample_layer(x, B, T, layer, w, norm_mix, b_sb, g_heads, norm_cross, norm_ffn, conv_w, conv_b,
                  cache_k, cache_v, page_table, c0, n0, m0, conv_state, mem_k, mem_v):
    M = B * T
    proj, gates = _mix_block(x, B, T, w, norm_mix, M, 1024)
    mix_ml, c, n, m = _mlstm(proj, gates, w["bg"], g_heads[:ML_WIDTH], B, T, T, (c0, n0, m0), F32)
    H, d = SB_HEADS, SB_DIM
    heads = lambda colblk: proj[:, colblk * SB_WIDTH:(colblk + 1) * SB_WIDTH].reshape(B, T, H, d)
    sbk, sbv = heads(COL_SK), heads(COL_SV)
    head_rows = lambda a: a.transpose(0, 2, 1, 3).reshape(B, H * T, d)
    bias_col = jnp.repeat(b_sb, T).reshape(H * T, 1)
    gn_rows = jnp.repeat(g_heads[ML_WIDTH:].reshape(H, d), T, axis=0)
    mix_sb = _sb_sample(head_rows(heads(COL_SQ)), head_rows(sbk), head_rows(sbv), cache_k, cache_v, layer,
                        page_table, bias_col, gn_rows, B, T)
    mix_sb = mix_sb.reshape(B, H, T, d).transpose(0, 2, 1, 3).reshape(M, SB_WIDTH)
    x = _out_proj(mix_ml, mix_sb, w["w_out"], x, M)
    n_mem = mem_k.shape[2]
    mem_spec = pl.BlockSpec((None, None, n_mem, X_WIDTH), lambda i: (layer, i, 0, 0))
    x = _cross(x, norm_cross, w["wx_q"], w["wx_o"], mem_k, mem_v, mem_spec, mem_spec, T)
    time_major = lambda a, n0, n1: a.reshape(n0, n1, -1).transpose(1, 0, 2).reshape(M, -1)
    x, st = _ffn(time_major(x, B, T), norm_ffn, w["w_up"], conv_w, conv_b, w["w_down"], B, T, M, 512,
                 buf=conv_state, layer=layer)
    x = time_major(x, T, B)
    return x, (sbk, sbv, c, n, m, st)


def kernel(x_prompt, x_sample, mem_prompt, cache_sb_k, cache_sb_v, state_mlstm_c, state_mlstm_n, state_mlstm_m,
           state_ffn_conv, cache_mem_k, cache_mem_v, page_table, norm_mix, w_in, b_gates, b_sb, norm_heads, w_out,
           norm_cross, norm_mem, wx_q, wx_kv, wx_o, norm_ffn, w_up, conv_w, conv_b, w_down, norm_final):
    Bp, Tp, D = x_prompt.shape
    Bs, Ts, _ = x_sample.shape
    depth = w_in.shape[0]
    n_mem = cache_mem_k.shape[2]
    n_pool = cache_sb_k.shape[1]
    conv_state = state_ffn_conv.transpose(0, 2, 1, 3)
    cache_k = cache_sb_k.reshape(depth, n_pool, PAGE * SB_HEADS, SB_DIM)
    cache_v = cache_sb_v.reshape(depth, n_pool, PAGE * SB_HEADS, SB_DIM)
    mem_k = cache_mem_k.reshape(depth, Bs, n_mem, X_WIDTH)
    mem_v = cache_mem_v.reshape(depth, Bs, n_mem, X_WIDTH)
    yp = x_prompt.reshape(Bp * Tp, D)
    ys = x_sample.reshape(Bs * Ts, D)
    mem = mem_prompt.reshape(Bp * mem_prompt.shape[1], D)
    P = [[] for _ in range(8)]
    S = [[] for _ in range(6)]
    for l in range(depth):
        w = _prep_layer_weights(w_in[l], b_gates[l], w_out[l], wx_q[l], wx_kv[l], wx_o[l], w_up[l], w_down[l])
        yp, st_p = _prompt_layer(yp, mem, Bp, Tp, w, norm_mix[l], b_sb[l], norm_heads[l], norm_cross[l],
                                 norm_mem[l], norm_ffn[l], conv_w[l], conv_b[l])
        for lst, a in zip(P, st_p):
            lst.append(a)
        ys, st_s = _sample_layer(ys, Bs, Ts, l, w, norm_mix[l], b_sb[l], norm_heads[l], norm_cross[l],
                                 norm_ffn[l], conv_w[l], conv_b[l], cache_k, cache_v, page_table,
                                 state_mlstm_c[l], state_mlstm_n[l], state_mlstm_m[l], conv_state,
                                 mem_k, mem_v)
        for lst, a in zip(S, st_s):
            lst.append(a)
    y_prompt = _final_norm(yp, norm_final, min(512, Bp * Tp)).reshape(Bp, Tp, D)
    y_sample = _final_norm(ys, norm_final, Bs * Ts).reshape(Bs, Ts, D)
    return (y_prompt, y_sample) + tuple(jnp.stack(a) for a in P) + tuple(jnp.stack(a) for a in S)
```

```python
import functools

import jax
import jax.numpy as jnp
from jax import lax
from jax.experimental import pallas as pl
from jax.experimental.pallas import tpu as pltpu

F32 = jnp.float32
BF16 = jnp.bfloat16

EPS = 1e-6
M_EMPTY = -1e30
ML_HEADS = 4
ML_DIM = 256
SB_HEADS = 8
SB_DIM = 128
X_HEADS = 4
X_DIM = 128
PAGE = 128
CONV_W = 3
ML_WIDTH = ML_HEADS * ML_DIM
SB_WIDTH = SB_HEADS * SB_DIM
X_WIDTH = X_HEADS * X_DIM

SUBLANES = 8
LANES = 128
MIB = 1024 * 1024
VMEM_LIMIT = 56 * MIB

COL_MQ, COL_MK, COL_MV, COL_MO = 0, 1, 2, 3
COL_SQ, COL_SK, COL_SV = 4, 5, 6
N_PROJ = 7 * ML_WIDTH

NT_DIMS = (((1,), (1,)), ((), ()))
TN_DIMS = (((0,), (0,)), ((), ()))


def _params(sem):
    return pltpu.CompilerParams(dimension_semantics=sem, vmem_limit_bytes=VMEM_LIMIT)


def _rmsnorm(xf, g):
    return xf * lax.rsqrt(jnp.mean(xf * xf, axis=-1, keepdims=True) + EPS) * g


def _log_sigmoid(x):
    return jnp.minimum(x, 0.0) - jnp.log1p(jnp.exp(-jnp.abs(x)))


def _softplus(z):
    t = jnp.exp(-jnp.abs(z))
    u = 1.0 + t
    return jnp.maximum(z, 0.0) + (jnp.log(u) + (t - (u - 1.0)))


def _split_dot(x, u):
    hi = x.astype(BF16)
    lo = (x - hi.astype(F32)).astype(BF16)
    return (jnp.dot(hi, u, preferred_element_type=F32) + jnp.dot(lo, u, preferred_element_type=F32))


def _norm_matmul_kernel(x_ref, g_ref, w_ref, o_ref, hn_ref):
    @pl.when(pl.program_id(1) == 0)
    def _():
        hn_ref[...] = _rmsnorm(x_ref[...], g_ref[...]).astype(BF16)

    o_ref[...] = jnp.dot(hn_ref[...], w_ref[...], preferred_element_type=F32)


def _norm_matmul(x, g, w, layer, tm, tn):
    M, D = x.shape
    N = w.shape[2]
    assert M % tm == 0 and N % tn == 0
    return pl.pallas_call(
        _norm_matmul_kernel,
        grid=(M // tm, N // tn),
        in_specs=[pl.BlockSpec((tm, D), lambda i, j: (i, 0)),
                  pl.BlockSpec((None, 1, D), lambda i, j: (layer, 0, 0)),
                  pl.BlockSpec((None, D, tn), lambda i, j: (layer, 0, j))],
        out_specs=pl.BlockSpec((tm, tn), lambda i, j: (i, j)),
        out_shape=jax.ShapeDtypeStruct((M, N), F32),
        scratch_shapes=[pltpu.VMEM((tm, D), BF16)],
        compiler_params=_params(("parallel", "arbitrary")),
        name="norm_matmul",
    )(x, g, w)


def _mlstm_kernel(q_ref, k_ref, v_ref, o_ref, gt_ref, bg_ref, gn_ref, *rest, L, has_init):
    if has_init:
        c0_ref, n0_ref, m0_ref, mix_ref, c_ref, n_ref, m_ref = rest
    else:
        mix_ref, c_ref, n_ref, m_ref = rest
    S = LANES
    H, d = ML_HEADS, ML_DIM

    @pl.when(pl.program_id(1) == 0)
    def _():
        if has_init:
            c_ref[...] = c0_ref[...]
            n_ref[...] = n0_ref[...]
            m_ref[...] = m0_ref[...]
        else:
            c_ref[...] = jnp.zeros(c_ref.shape, F32)
            n_ref[...] = jnp.zeros(n_ref.shape, F32)
            m_ref[...] = jnp.full(m_ref.shape, M_EMPTY, F32)

    def pad(a):
        if L == S:
            return a
        return jnp.concatenate([a, jnp.zeros((S - L, a.shape[1]), a.dtype)], axis=0)

    gates = pad(gt_ref[...] + bg_ref[...])
    row = lax.broadcasted_iota(jnp.int32, (S, S), 0)
    col = lax.broadcasted_iota(jnp.int32, (S, S), 1)
    causal = col <= row
    valid = lax.broadcasted_iota(jnp.int32, (S, 1), 0) < L
    b_all = jnp.dot(causal.astype(F32), _log_sigmoid(gates), precision=lax.Precision.HIGHEST,
                    preferred_element_type=F32)
    r_t = (gates - pltpu.roll(b_all, LANES - H, axis=1)).T

    for h in range(H):
        hs = slice(h * d, (h + 1) * d)
        i_col = gates[:, h:h + 1]
        b_col = b_all[:, H + h:H + h + 1]
        g_col = b_col + m_ref[h]
        logd = b_col + r_t[h:h + 1, :]
        mt = jnp.maximum(g_col, jnp.max(jnp.where(causal, logd, -jnp.inf), axis=-1, keepdims=True))

        q = pad(q_ref[:, hs])
        k = pad(k_ref[:, hs]) * (d ** -0.5)
        v = pad(v_ref[:, hs])
        qb, kb = q.astype(BF16), k.astype(BF16)
        s = lax.dot_general(q, k, NT_DIMS, precision=lax.Precision.HIGHEST, preferred_element_type=F32)
        s = jnp.where(causal, s * jnp.exp(logd - mt), 0.0)
        gw = jnp.exp(g_col - mt)
        C = c_ref[0, h]
        n_row = n_ref[h]
        num = (jnp.dot(s.astype(BF16), v.astype(BF16), preferred_element_type=F32)
               + gw * jnp.dot(qb, C.astype(BF16), preferred_element_type=F32))
        den = jnp.sum(s, axis=-1, keepdims=True) + gw * jnp.sum(q * n_row, axis=-1, keepdims=True)
        den = jnp.maximum(jnp.abs(den), jnp.exp(-mt))
        hcell = num / den

        m_new = mt[L - 1:L, :]
        wk = jnp.where(valid, jnp.exp(b_col[L - 1:L, :] - b_col + i_col - m_new), 0.0)
        decay = jnp.exp(g_col[L - 1:L, :] - m_new)
        c_ref[0, h] = decay * C + lax.dot_general(kb, (wk * v).astype(BF16), TN_DIMS,
                                                  preferred_element_type=F32)
        n_ref[h] = decay * n_row + jnp.sum(wk * k, axis=0, keepdims=True)
        m_ref[h] = m_new

        hm = jax.nn.sigmoid(o_ref[:, hs]) * hcell[:L]
        mix_ref[:, hs] = _rmsnorm(hm, gn_ref[:, hs]).astype(mix_ref.dtype)


def _mlstm(proj, gates, b_gates, g_heads, layer, B, T, L, init, mix_dtype):
    nc = T // L
    H, d = ML_HEADS, ML_DIM
    W = ML_WIDTH
    has_init = init is not None

    def qkvo(colblk):
        return pl.BlockSpec((L, W), lambda b, c: (b * nc + c, colblk))

    state_specs = [pl.BlockSpec((1, H, d, d), lambda b, c: (b, 0, 0, 0)),
                   pl.BlockSpec((H, 1, d), lambda b, c: (b, 0, 0)),
                   pl.BlockSpec((H, 1, 1), lambda b, c: (b, 0, 0))]
    in_specs = [qkvo(COL_MQ), qkvo(COL_MK), qkvo(COL_MV), qkvo(COL_MO),
                pl.BlockSpec((L, LANES), lambda b, c: (b * nc + c, 0)),
                pl.BlockSpec((None, 1, LANES), lambda b, c: (layer, 0, 0)),
                pl.BlockSpec((None, 1, W), lambda b, c: (layer, 0, 0))]
    args = [proj, proj, proj, proj, gates, b_gates, g_heads]
    if has_init:
        c0, n0, m0 = init
        depth = c0.shape[0]
        in_specs += [pl.BlockSpec((None, 1, H, d, d), lambda b, c: (layer, b, 0, 0, 0)),
                     pl.BlockSpec((None, H, 1, d), lambda b, c: (layer, b, 0, 0)),
                     pl.BlockSpec((None, H, 1, 1), lambda b, c: (layer, b, 0, 0))]
        args += [c0, n0.reshape(depth, B * H, 1, d), m0.reshape(depth, B * H, 1, 1)]
    mix, c, n, m = pl.pallas_call(
        functools.partial(_mlstm_kernel, L=L, has_init=has_init),
        grid=(B, nc),
        in_specs=in_specs,
        out_specs=[pl.BlockSpec((L, W), lambda b, c: (b * nc + c, 0))] + state_specs,
        out_shape=[jax.ShapeDtypeStruct((B * T, W), mix_dtype),
                   jax.ShapeDtypeStruct((B, H, d, d), F32),
                   jax.ShapeDtypeStruct((B * H, 1, d), F32),
                   jax.ShapeDtypeStruct((B * H, 1, 1), F32)],
        compiler_params=_params(("parallel", "arbitrary")),
        name="mlstm",
    )(*args)
    return mix, c, n.reshape(B, H, d), m.reshape(B, H)


def _sb_block(z, u, carry, mask):
    sp = _softplus(z)
    lk = -sp
    if mask is not None:
        lk = jnp.where(mask, lk, 0.0)
    after = _split_dot(lk, u)
    a = jnp.exp((z - sp) + after + carry)
    if mask is not None:
        a = jnp.where(mask, a, 0.0)
    return a, carry + (after[:, 0:1] + lk[:, 0:1])


def _sb_prompt_kernel(bias_ref, q_ref, k_ref, v_ref, gn_ref, o_ref, kb_ref, vb_ref, ls_ref, lk_ref, e_ref, *,
                      TQ, layer):
    h = pl.program_id(1)
    i = pl.program_id(2)

    @pl.when(i == 0)
    def _():
        kb_ref[...] = k_ref[...].astype(BF16)
        vb_ref[...] = v_ref[...].astype(BF16)

    bias = bias_ref[layer, h]
    scale = SB_DIM ** -0.5
    qb = q_ref[...].astype(BF16)
    row = lax.broadcasted_iota(jnp.int32, (TQ, TQ), 0)
    col = lax.broadcasted_iota(jnp.int32, (TQ, TQ), 1)
    u = (row > col).astype(BF16)

    def rows(j):
        return pl.ds(pl.multiple_of(jnp.maximum(j, 0) * TQ, TQ), TQ)

    def logits(j):
        return lax.dot_general(qb, kb_ref[rows(j), :], NT_DIMS, preferred_element_type=F32) * scale + bias

    def store_log_gates(z, mask):
        sp = _softplus(z)
        ls_ref[...] = z - sp
        lk_ref[...] = -sp if mask is None else jnp.where(mask, -sp, 0.0)

    def exponent(mask):
        lk = lk_ref[...]
        after = _split_dot(lk, u)
        e = ls_ref[...] + after
        if mask is not None:
            e = jnp.where(mask, e, M_EMPTY)
        return e, after[:, 0:1] + lk[:, 0:1]

    store_log_gates(logits(i), col < row)
    z1 = logits(i - 1)
    e0, tot0 = exponent(col < row)
    e_ref[...] = e0
    store_log_gates(z1, None)

    def body(t, st):
        tot, carry, acc = st
        j = i - t
        z = logits(j - 2)
        a = jnp.exp(e_ref[...] + carry)
        acc = acc + jnp.dot(a.astype(BF16), vb_ref[rows(j), :], preferred_element_type=F32)
        e_new, tot_new = exponent(None)
        e_ref[...] = e_new
        store_log_gates(z, None)
        return tot_new, carry + tot, acc

    _, _, acc = lax.fori_loop(0, i + 1, body, (tot0, jnp.zeros((TQ, 1), F32), jnp.zeros((TQ, SB_DIM), F32)))
    o_ref[...] = _rmsnorm(acc, gn_ref[...]).astype(o_ref.dtype)


def _sb_prompt(proj, b_sb, g_heads, layer, B, T):
    TQ = min(256, T)
    nq = T // TQ
    H, d = SB_HEADS, SB_DIM
    return pl.pallas_call(
        functools.partial(_sb_prompt_kernel, TQ=TQ, layer=layer),
        grid=(B, H, nq),
        in_specs=[pl.BlockSpec(memory_space=pltpu.SMEM),
                  pl.BlockSpec((TQ, d), lambda b, h, i: (b * nq + i, COL_SQ * H + h)),
                  pl.BlockSpec((T, d), lambda b, h, i: (b, COL_SK * H + h)),
                  pl.BlockSpec((T, d), lambda b, h, i: (b, COL_SV * H + h)),
                  pl.BlockSpec((None, 1, d), lambda b, h, i: (layer, 0, ML_WIDTH // d + h))],
        out_specs=pl.BlockSpec((TQ, d), lambda b, h, i: (b * nq + i, h)),
        out_shape=jax.ShapeDtypeStruct((B * T, SB_WIDTH), BF16),
        scratch_shapes=[pltpu.VMEM((T, d), BF16), pltpu.VMEM((T, d), BF16),
                        pltpu.VMEM((TQ, TQ), F32), pltpu.VMEM((TQ, TQ), F32), pltpu.VMEM((TQ, TQ), F32)],
        compiler_params=_params(("parallel", "parallel", "arbitrary")),
        name="sb_prompt",
    )(b_sb, proj, proj, proj, g_heads)


def _sb_sample_kernel(pt_ref, q_ref, kn_ref, vn_ref, bias_ref, gn_ref, *rest, T, G):
    k_refs, v_refs = rest[:G], rest[G:2 * G]
    o_ref, acc_ref, carry_ref = rest[2 * G:]
    H, d = SB_HEADS, SB_DIM
    R = H * T
    s_id = pl.program_id(1)
    scale = d ** -0.5
    row = lax.broadcasted_iota(jnp.int32, (PAGE, PAGE), 0)
    col = lax.broadcasted_iota(jnp.int32, (PAGE, PAGE), 1)
    u = (row > col).astype(BF16)
    qb = q_ref[...].astype(BF16)
    bias = bias_ref[...]

    def logits(keys_of_head):
        z = [lax.dot_general(qb[h * T:(h + 1) * T], keys_of_head(h), NT_DIMS, preferred_element_type=F32)
             for h in range(H)]
        return jnp.concatenate(z, axis=0) * scale + bias

    def weighted_values(a, values_of_head):
        ab = a.astype(BF16)
        o = [jnp.dot(ab[h * T:(h + 1) * T], values_of_head(h), preferred_element_type=F32) for h in range(H)]
        return jnp.concatenate(o, axis=0)

    @pl.when(s_id == 0)
    def _():
        zpad = jnp.zeros((PAGE - T, d), F32)
        new_rows = lambda ref: lambda h: jnp.concatenate([ref[h * T:(h + 1) * T, :], zpad], axis=0).astype(BF16)
        z = logits(new_rows(kn_ref))
        mask = (lax.broadcasted_iota(jnp.int32, (R, PAGE), 1) < lax.broadcasted_iota(jnp.int32, (R, PAGE), 0) % T)
        a, carry = _sb_block(z, u, jnp.zeros((R, 1), F32), mask)
        carry_ref[...] = carry
        acc_ref[...] = weighted_values(a, new_rows(vn_ref))

    def page_rows(refs):
        return lambda h: jnp.concatenate([r[pl.ds(h, PAGE, stride=H), :] for r in refs], axis=0).astype(BF16)

    z = logits(page_rows(k_refs))
    sp = _softplus(z)
    lk = -sp
    lk_st = jnp.concatenate([lk[:, g * PAGE:(g + 1) * PAGE] for g in range(G)], axis=0)
    after_st = _split_dot(lk_st, u)
    tot_st = after_st[:, 0:1] + lk_st[:, 0:1]
    carry = carry_ref[...]
    after = []
    for g in range(G):
        after.append(after_st[g * R:(g + 1) * R] + carry)
        carry = carry + tot_st[g * R:(g + 1) * R]
    carry_ref[...] = carry
    a = jnp.exp((z - sp) + jnp.concatenate(after, axis=1))
    acc_ref[...] += weighted_values(a, page_rows(v_refs))

    @pl.when(s_id == pl.num_programs(1) - 1)
    def _():
        o_ref[...] = _rmsnorm(acc_ref[...], gn_ref[...])


def _sb_sample(q, k_new, v_new, cache_k, cache_v, layer, page_table, bias_col, gn_rows, B, T):
    n_pages = page_table.shape[1]
    G = 8 if n_pages % 8 == 0 else 1
    nsteps = n_pages // G
    H, d = SB_HEADS, SB_DIM
    R = H * T

    def page_spec(g):
        return pl.BlockSpec((None, None, PAGE * H, d),
                            lambda b, s, pt: (layer, pt[b, n_pages - 1 - (s * G + g)], 0, 0))

    row_spec = pl.BlockSpec((None, R, d), lambda b, s, pt: (b, 0, 0))
    grid_spec = pltpu.PrefetchScalarGridSpec(
        num_scalar_prefetch=1,
        grid=(B, nsteps),
        in_specs=[row_spec, row_spec, row_spec,
                  pl.BlockSpec((None, R, 1), lambda b, s, pt: (layer, 0, 0)),
                  pl.BlockSpec((None, R, d), lambda b, s, pt: (layer, 0, 0))]
                 + [page_spec(g) for g in range(G)] + [page_spec(g) for g in range(G)],
        out_specs=row_spec,
        scratch_shapes=[pltpu.VMEM((R, d), F32), pltpu.VMEM((R, 1), F32)],
    )
    return pl.pallas_call(
        functools.partial(_sb_sample_kernel, T=T, G=G),
        grid_spec=grid_spec,
        out_shape=jax.ShapeDtypeStruct((B, R, d), F32),
        compiler_params=_params(("parallel", "arbitrary")),
        name="sb_sample",
    )(page_table, q, k_new, v_new, bias_col, gn_rows, *([cache_k] * G), *([cache_v] * G))


def _out_proj_kernel(a1_ref, a2_ref, w1_ref, w2_ref, x_ref, o_ref):
    o_ref[...] = (x_ref[...]
                  + jnp.dot(a1_ref[...].astype(BF16), w1_ref[...], preferred_element_type=F32)
                  + jnp.dot(a2_ref[...].astype(BF16), w2_ref[...], preferred_element_type=F32))


def _out_proj(a1, a2, w, layer, x, tm):
    M, D = x.shape
    K1, K2 = a1.shape[1], a2.shape[1]
    return pl.pallas_call(
        _out_proj_kernel,
        grid=(M // tm,),
        in_specs=[pl.BlockSpec((tm, K1), lambda i: (i, 0)),
                  pl.BlockSpec((tm, K2), lambda i: (i, 0)),
                  pl.BlockSpec((None, K1, D), lambda i: (layer, 0, 0)),
                  pl.BlockSpec((None, K2, D), lambda i: (layer, K1 // K2, 0)),
                  pl.BlockSpec((tm, D), lambda i: (i, 0))],
        out_specs=pl.BlockSpec((tm, D), lambda i: (i, 0)),
        out_shape=jax.ShapeDtypeStruct((M, D), F32),
        compiler_params=_params(("parallel",)),
        name="out_proj",
    )(a1, a2, w, w, x)


def _cross_kernel(x_ref, g_ref, wq_ref, mk_ref, mv_ref, wo_ref, o_ref):
    x = x_ref[...]
    hn = _rmsnorm(x, g_ref[...]).astype(BF16)
    q = jnp.dot(hn, wq_ref[...], preferred_element_type=F32)
    scale = X_DIM ** -0.5
    outs = []
    for h in range(X_HEADS):
        sl = slice(h * X_DIM, (h + 1) * X_DIM)
        s = lax.dot_general(q[:, sl].astype(BF16), mk_ref[:, sl].astype(BF16), NT_DIMS,
                            preferred_element_type=F32) * scale
        e = jnp.exp(s - jnp.max(s, axis=-1, keepdims=True))
        p = e / jnp.sum(e, axis=-1, keepdims=True)
        outs.append(jnp.dot(p.astype(BF16), mv_ref[:, sl].astype(BF16), preferred_element_type=F32))
    o = jnp.concatenate(outs, axis=1).astype(BF16)
    o_ref[...] = x + jnp.dot(o, wo_ref[...], preferred_element_type=F32)


def _cross(x, g, wq, wo, layer, mk, mv, mk_spec, mv_spec, tm):
    M, D = x.shape
    return pl.pallas_call(
        _cross_kernel,
        grid=(M // tm,),
        in_specs=[pl.BlockSpec((tm, D), lambda i: (i, 0)),
                  pl.BlockSpec((None, 1, D), lambda i: (layer, 0, 0)),
                  pl.BlockSpec((None, D, X_WIDTH), lambda i: (layer, 0, 0)),
                  mk_spec, mv_spec,
                  pl.BlockSpec((None, X_WIDTH, D), lambda i: (layer, 0, 0))],
        out_specs=pl.BlockSpec((tm, D), lambda i: (i, 0)),
        out_shape=jax.ShapeDtypeStruct((M, D), F32),
        compiler_params=_params(("parallel",)),
        name="cross",
    )(x, g, wq, mk, mv, wo)


def _ffn_kernel(*refs, tm, from_buf, tiles_per_seq):
    if from_buf:
        x_ref, g_ref, wa_ref, wb_ref, cw_ref, cb_ref, wd_ref, buf_ref, o_ref, st_ref, hn_ref, acc_ref = refs
    else:
        x_ref, xh_ref, g_ref, wa_ref, wb_ref, cw_ref, cb_ref, wd_ref, o_ref, st_ref, hn_ref, acc_ref = refs
    HALO = SUBLANES
    f = pl.program_id(1)

    @pl.when(f == 0)
    def _():
        hn_ref[HALO:, :] = _rmsnorm(x_ref[...], g_ref[...]).astype(BF16)
        if not from_buf:
            hn_ref[:HALO, :] = _rmsnorm(xh_ref[...], g_ref[...]).astype(BF16)
        acc_ref[...] = jnp.zeros(acc_ref.shape, F32)

    tf = wa_ref.shape[1]
    if from_buf:
        a = jnp.dot(hn_ref[HALO:, :], wa_ref[...], preferred_element_type=F32)
        nb = buf_ref.shape[1]
        a_ext = jnp.concatenate([buf_ref[j] for j in range(CONV_W - 1)] + [a], axis=0)
        taps = [a_ext[j * nb:j * nb + tm, :] for j in range(CONV_W)]
    else:
        a_ext = jnp.dot(hn_ref[...], wa_ref[...], preferred_element_type=F32)
        seq_start = (pl.program_id(0) % tiles_per_seq) == 0
        rows = lax.broadcasted_iota(jnp.int32, (HALO + tm, 1), 0)
        a_ext = jnp.where(jnp.logical_and(seq_start, rows < HALO), 0.0, a_ext)
        taps = [pltpu.roll(a_ext, 2, axis=0)[HALO:, :], pltpu.roll(a_ext, 1, axis=0)[HALO:, :], a_ext[HALO:, :]]
    b = jnp.dot(hn_ref[HALO:, :], wb_ref[...], preferred_element_type=F32)
    c = cb_ref[...] + ((taps[0] * cw_ref[0:1, :] + taps[1] * cw_ref[1:2, :]) + taps[2] * cw_ref[2:3, :])
    gate = (c * jax.nn.sigmoid(c)) * b
    acc_ref[...] += jnp.dot(gate.astype(BF16), wd_ref[...], preferred_element_type=F32)
    st_ref[...] = taps[2][tm - st_ref.shape[0]:, :]

    @pl.when(f == pl.num_programs(1) - 1)
    def _():
        o_ref[...] = x_ref[...] + acc_ref[...]


def _ffn(x, g, w_up, conv_w, conv_b, w_down, layer, B, T, tm, tf, buf=None):
    M, D = x.shape
    F = w_down.shape[1]
    nf = F // tf
    from_buf = buf is not None
    assert F % tf == 0 and tm % SUBLANES == 0 and (tm == M and B % SUBLANES == 0 if from_buf else T % tm == 0)
    tiles_per_seq = 1 if from_buf else T // tm
    st_rows = (CONV_W - 1) * B if from_buf else SUBLANES
    hb = tm // SUBLANES
    in_specs = [pl.BlockSpec((tm, D), lambda i, f: (i, 0))]
    args = [x]
    if not from_buf:
        in_specs.append(pl.BlockSpec((SUBLANES, D), lambda i, f: (jnp.maximum(i * hb - 1, 0), 0)))
        args.append(x)
    in_specs += [pl.BlockSpec((None, 1, D), lambda i, f: (layer, 0, 0)),
                 pl.BlockSpec((None, D, tf), lambda i, f: (layer, 0, f)),
                 pl.BlockSpec((None, D, tf), lambda i, f: (layer, 0, nf + f)),
                 pl.BlockSpec((None, CONV_W, tf), lambda i, f: (layer, 0, f)),
                 pl.BlockSpec((None, 1, tf), lambda i, f: (layer, 0, f)),
                 pl.BlockSpec((None, tf, D), lambda i, f: (layer, f, 0))]
    args += [g, w_up, w_up, conv_w, conv_b, w_down]
    if from_buf:
        in_specs.append(pl.BlockSpec((None, CONV_W - 1, B, tf), lambda i, f: (layer, 0, 0, f)))
        args.append(buf)
    y, st = pl.pallas_call(
        functools.partial(_ffn_kernel, tm=tm, from_buf=from_buf, tiles_per_seq=tiles_per_seq),
        grid=(M // tm, nf),
        in_specs=in_specs,
        out_specs=[pl.BlockSpec((tm, D), lambda i, f: (i, 0)),
                   pl.BlockSpec((None, st_rows, tf), lambda i, f: (i, 0, f))],
        out_shape=[jax.ShapeDtypeStruct((M, D), F32),
                   jax.ShapeDtypeStruct((M // tm, st_rows, F), F32)],
        scratch_shapes=[pltpu.VMEM((SUBLANES + tm, D), BF16), pltpu.VMEM((tm, D), F32)],
        compiler_params=_params(("parallel", "arbitrary")),
        name="ffn",
    )(*args)
    if from_buf:
        return y, st.reshape(CONV_W - 1, B, F).transpose(1, 0, 2)
    return y, st[tiles_per_seq - 1::tiles_per_seq, SUBLANES - (CONV_W - 1):, :]


def _norm_kernel(x_ref, g_ref, o_ref):
    o_ref[...] = _rmsnorm(x_ref[...], g_ref[...])


def _final_norm(x, g, tm):
    M, D = x.shape
    return pl.pallas_call(
        _norm_kernel,
        grid=(M // tm,),
        in_specs=[pl.BlockSpec((tm, D), lambda i: (i, 0)), pl.BlockSpec((1, D), lambda i: (0, 0))],
        out_specs=pl.BlockSpec((tm, D), lambda i: (i, 0)),
        out_shape=jax.ShapeDtypeStruct((M, D), F32),
        compiler_params=_params(("parallel",)),
        name="final_norm",
    )(x, g.reshape(1, D))


def _prep_params(T_s, w_in, b_gates, b_sb, norm_mix, norm_heads, w_out, norm_cross, norm_mem, wx_q, wx_kv, wx_o,
                 norm_ffn, w_up, conv_w, conv_b, w_down):
    depth, D, _ = w_in.shape
    H, d = SB_HEADS, SB_DIM
    g0 = 4 * ML_WIDTH
    g1 = g0 + 2 * ML_HEADS
    pad = LANES - 2 * ML_HEADS
    row = lambda a: a.reshape(depth, 1, -1)
    return dict(
        w_main=jnp.concatenate([w_in[:, :, :g0].astype(BF16), w_in[:, :, g1:].astype(BF16)], axis=2),
        w_gate=jnp.concatenate([w_in[:, :, g0:g1], jnp.zeros((depth, D, pad), F32)], axis=2).astype(BF16),
        bg=row(jnp.concatenate([b_gates, jnp.zeros((depth, pad), F32)], axis=1)),
        w_out=w_out.astype(BF16), wx_q=wx_q.astype(BF16), wx_kv=wx_kv.astype(BF16), wx_o=wx_o.astype(BF16),
        w_up=w_up.astype(BF16), w_down=w_down.astype(BF16),
        norm_mix=row(norm_mix), norm_heads=row(norm_heads), norm_cross=row(norm_cross), norm_mem=row(norm_mem),
        norm_ffn=row(norm_ffn), conv_w=conv_w, conv_b=row(conv_b), b_sb=b_sb,
        bias_col=jnp.repeat(b_sb, T_s, axis=1).reshape(depth, H * T_s, 1),
        gn_rows=jnp.repeat(norm_heads[:, ML_WIDTH:].reshape(depth, H, d), T_s, axis=1))


def _mix_inputs(x, layer, p, tm):
    proj = _norm_matmul(x, p["norm_mix"], p["w_main"], layer, tm, 1024)
    gates = _norm_matmul(x, p["norm_mix"], p["w_gate"], layer, tm, LANES)
    return proj, gates


def _prompt_layer(x, mem, B, T, layer, p):
    n_mem = mem.shape[0] // B
    kv = _norm_matmul(mem, p["norm_mem"], p["wx_kv"], layer, min(512, mem.shape[0]), 512)
    tm = min(512, T)
    proj, gates = _mix_inputs(x, layer, p, min(1024, T))
    mix_ml, c, n, m = _mlstm(proj, gates, p["bg"], p["norm_heads"], layer, B, T, min(LANES, T), None, BF16)
    mix_sb = _sb_prompt(proj, p["b_sb"], p["norm_heads"], layer, B, T)
    x = _out_proj(mix_ml, mix_sb, p["w_out"], layer, x, tm)
    tpb = T // tm
    x = _cross(x, p["norm_cross"], p["wx_q"], p["wx_o"], layer, kv, kv,
               pl.BlockSpec((n_mem, X_WIDTH), lambda i: (i // tpb, 0)),
               pl.BlockSpec((n_mem, X_WIDTH), lambda i: (i // tpb, 1)), tm)
    x, st = _ffn(x, p["norm_ffn"], p["w_up"], p["conv_w"], p["conv_b"], p["w_down"], layer, B, T, tm, 512)
    return x, (proj, c, n, m, st, kv)


def _sample_layer(x, B, T, layer, p, cache_k, cache_v, page_table, state, conv_state, mem_k, mem_v):
    M = B * T
    H, d = SB_HEADS, SB_DIM
    proj, gates = _mix_inputs(x, layer, p, M)
    mix_ml, c, n, m = _mlstm(proj, gates, p["bg"], p["norm_heads"], layer, B, T, T, state, F32)
    heads = lambda colblk: proj[:, colblk * SB_WIDTH:(colblk + 1) * SB_WIDTH].reshape(B, T, H, d)
    sbk, sbv = heads(COL_SK), heads(COL_SV)
    head_rows = lambda a: a.transpose(0, 2, 1, 3).reshape(B, H * T, d)
    mix_sb = _sb_sample(head_rows(heads(COL_SQ)), head_rows(sbk), head_rows(sbv), cache_k, cache_v, layer,
                        page_table, p["bias_col"], p["gn_rows"], B, T)
    mix_sb = mix_sb.reshape(B, H, T, d).transpose(0, 2, 1, 3).reshape(M, SB_WIDTH)
    x = _out_proj(mix_ml, mix_sb, p["w_out"], layer, x, M)
    n_mem = mem_k.shape[2]
    mem_spec = pl.BlockSpec((None, None, n_mem, X_WIDTH), lambda i: (layer, i, 0, 0))
    x = _cross(x, p["norm_cross"], p["wx_q"], p["wx_o"], layer, mem_k, mem_v, mem_spec, mem_spec, T)
    time_major = lambda a, n0, n1: a.reshape(n0, n1, -1).transpose(1, 0, 2).reshape(M, -1)
    x, st = _ffn(time_major(x, B, T), p["norm_ffn"], p["w_up"], p["conv_w"], p["conv_b"], p["w_down"], layer,
                 B, T, M, 512, buf=conv_state)
    x = time_major(x, T, B)
    return x, (sbk, sbv, c, n, m, st)


def kernel(x_prompt, x_sample, mem_prompt, cache_sb_k, cache_sb_v, state_mlstm_c, state_mlstm_n, state_mlstm_m,
           state_ffn_conv, cache_mem_k, cache_mem_v, page_table, norm_mix, w_in, b_gates, b_sb, norm_heads, w_out,
           norm_cross, norm_mem, wx_q, wx_kv, wx_o, norm_ffn, w_up, conv_w, conv_b, w_down, norm_final):
    Bp, Tp, D = x_prompt.shape
    Bs, Ts, _ = x_sample.shape
    depth = w_in.shape[0]
    n_mem = cache_mem_k.shape[2]
    n_pool = cache_sb_k.shape[1]
    p = _prep_params(Ts, w_in, b_gates, b_sb, norm_mix, norm_heads, w_out, norm_cross, norm_mem, wx_q, wx_kv, wx_o,
                     norm_ffn, w_up, conv_w, conv_b, w_down)
    state = (state_mlstm_c, state_mlstm_n, state_mlstm_m)
    conv_state = state_ffn_conv.transpose(0, 2, 1, 3)
    cache_k = cache_sb_k.reshape(depth, n_pool, PAGE * SB_HEADS, SB_DIM)
    cache_v = cache_sb_v.reshape(depth, n_pool, PAGE * SB_HEADS, SB_DIM)
    mem_k = cache_mem_k.reshape(depth, Bs, n_mem, X_WIDTH)
    mem_v = cache_mem_v.reshape(depth, Bs, n_mem, X_WIDTH)
    yp = x_prompt.reshape(Bp * Tp, D)
    ys = x_sample.reshape(Bs * Ts, D)
    mem = mem_prompt.reshape(Bp * mem_prompt.shape[1], D)
    P = [[] for _ in range(6)]
    S = [[] for _ in range(6)]
    for l in range(depth):
        yp, st_p = _prompt_layer(yp, mem, Bp, Tp, l, p)
        for lst, a in zip(P, st_p):
            lst.append(a)
        ys, st_s = _sample_layer(ys, Bs, Ts, l, p, cache_k, cache_v, page_table, state, conv_state, mem_k, mem_v)
        for lst, a in zip(S, st_s):
            lst.append(a)
    y_prompt = _final_norm(yp, norm_final, min(512, Bp * Tp)).reshape(Bp, Tp, D)
    y_sample = _final_norm(ys, norm_final, Bs * Ts).reshape(Bs, Ts, D)
    projs, pc, pn, pm, pconv, kvs = P
    cols = lambda a, j: jnp.stack([x[:, j * SB_WIDTH:(j + 1) * SB_WIDTH] for x in a])
    p_sb_k = cols(projs, COL_SK).reshape(depth, Bp, Tp, SB_HEADS, SB_DIM)
    p_sb_v = cols(projs, COL_SV).reshape(depth, Bp, Tp, SB_HEADS, SB_DIM)
    kv = jnp.stack(kvs)
    p_mem_k = kv[:, :, :X_WIDTH].reshape(depth, Bp, n_mem, X_HEADS, X_DIM)
    p_mem_v = kv[:, :, X_WIDTH:].reshape(depth, Bp, n_mem, X_HEADS, X_DIM)
    return ((y_prompt, y_sample, p_sb_k, p_sb_v, jnp.stack(pc), jnp.stack(pn), jnp.stack(pm), jnp.stack(pconv),
             p_mem_k, p_mem_v) + tuple(jnp.stack(a) for a in S))
```

```python
import functools

import jax
import jax.numpy as jnp
from jax import lax
from jax.experimental import pallas as pl
from jax.experimental.pallas import tpu as pltpu

F32 = jnp.float32
BF16 = jnp.bfloat16

EPS = 1e-6
M_EMPTY = -1e30
ML_HEADS = 4
ML_DIM = 256
SB_HEADS = 8
SB_DIM = 128
X_HEADS = 4
X_DIM = 128
PAGE = 128
CONV_W = 3
ML_WIDTH = ML_HEADS * ML_DIM
SB_WIDTH = SB_HEADS * SB_DIM
X_WIDTH = X_HEADS * X_DIM

SUBLANES = 8
LANES = 128
MIB = 1024 * 1024
VMEM_LIMIT = 56 * MIB

COL_MQ, COL_MK, COL_MV, COL_MO = 0, 1, 2, 3
COL_SQ, COL_SK, COL_SV = 4, 5, 6
N_PROJ = 7 * ML_WIDTH

NT_DIMS = (((1,), (1,)), ((), ()))
TN_DIMS = (((0,), (0,)), ((), ()))


def _params(sem):
    return pltpu.CompilerParams(dimension_semantics=sem, vmem_limit_bytes=VMEM_LIMIT)


def _rmsnorm(xf, g):
    return xf * lax.rsqrt(jnp.mean(xf * xf, axis=-1, keepdims=True) + EPS) * g


def _log_sigmoid(x):
    return jnp.minimum(x, 0.0) - jnp.log1p(jnp.exp(-jnp.abs(x)))


def _softplus(z):
    t = jnp.exp(-jnp.abs(z))
    u = 1.0 + t
    return jnp.maximum(z, 0.0) + (jnp.log(u) + (t - (u - 1.0)))


def _split_dot(x, u):
    hi = x.astype(BF16)
    lo = (x - hi.astype(F32)).astype(BF16)
    return (jnp.dot(hi, u, preferred_element_type=F32) + jnp.dot(lo, u, preferred_element_type=F32))


def _norm_matmul_kernel(x_ref, g_ref, w_ref, o_ref, hn_ref):
    @pl.when(pl.program_id(1) == 0)
    def _():
        hn_ref[...] = _rmsnorm(x_ref[...], g_ref[...]).astype(BF16)

    o_ref[...] = jnp.dot(hn_ref[...], w_ref[...], preferred_element_type=F32)


def _norm_matmul(x, g, w, layer, tm, tn):
    M, D = x.shape
    N = w.shape[2]
    assert M % tm == 0 and N % tn == 0
    return pl.pallas_call(
        _norm_matmul_kernel,
        grid=(M // tm, N // tn),
        in_specs=[pl.BlockSpec((tm, D), lambda i, j: (i, 0)),
                  pl.BlockSpec((None, 1, D), lambda i, j: (layer, 0, 0)),
                  pl.BlockSpec((None, D, tn), lambda i, j: (layer, 0, j))],
        out_specs=pl.BlockSpec((tm, tn), lambda i, j: (i, j)),
        out_shape=jax.ShapeDtypeStruct((M, N), F32),
        scratch_shapes=[pltpu.VMEM((tm, D), BF16)],
        compiler_params=_params(("parallel", "arbitrary")),
        name="norm_matmul",
    )(x, g, w)


def _in_proj_kernel(x_ref, g_ref, wml_ref, wsb_ref, wg_ref, o_ref, og_ref, hn_ref, *, n_ml):
    n = pl.program_id(1)

    @pl.when(n == 0)
    def _():
        hn_ref[...] = _rmsnorm(x_ref[...], g_ref[...]).astype(BF16)
        og_ref[...] = jnp.dot(hn_ref[...], wg_ref[...], preferred_element_type=F32)

    @pl.when(n < n_ml)
    def _():
        o_ref[...] = jnp.dot(hn_ref[...], wml_ref[...], preferred_element_type=F32)

    @pl.when(n >= n_ml)
    def _():
        o_ref[...] = jnp.dot(hn_ref[...], wsb_ref[...], preferred_element_type=F32)


def _in_proj(x, g, w_ml, w_sb, w_gate, layer, tm, tn):
    M, D = x.shape
    n_ml, n_sb = w_ml.shape[2] // tn, w_sb.shape[2] // tn
    assert M % tm == 0 and w_ml.shape[2] % tn == 0 and w_sb.shape[2] % tn == 0
    return pl.pallas_call(
        functools.partial(_in_proj_kernel, n_ml=n_ml),
        grid=(M // tm, n_ml + n_sb),
        in_specs=[pl.BlockSpec((tm, D), lambda i, j: (i, 0)),
                  pl.BlockSpec((None, 1, D), lambda i, j: (layer, 0, 0)),
                  pl.BlockSpec((None, D, tn), lambda i, j: (layer, 0, jnp.minimum(j, n_ml - 1))),
                  pl.BlockSpec((None, D, tn), lambda i, j: (layer, 0, jnp.maximum(j - n_ml, 0))),
                  pl.BlockSpec((None, D, LANES), lambda i, j: (layer, 0, 0))],
        out_specs=[pl.BlockSpec((tm, tn), lambda i, j: (i, j)),
                   pl.BlockSpec((tm, LANES), lambda i, j: (i, 0))],
        out_shape=[jax.ShapeDtypeStruct((M, (n_ml + n_sb) * tn), F32),
                   jax.ShapeDtypeStruct((M, LANES), F32)],
        scratch_shapes=[pltpu.VMEM((tm, D), BF16)],
        compiler_params=_params(("parallel", "arbitrary")),
        name="in_proj",
    )(x, g, w_ml, w_sb, w_gate)


def _mlstm_kernel(q_ref, k_ref, v_ref, o_ref, gt_ref, bg_ref, gn_ref, *rest, L, has_init):
    if has_init:
        c0_ref, n0_ref, m0_ref, mix_ref, c_ref, n_ref, m_ref = rest
    else:
        mix_ref, c_ref, n_ref, m_ref = rest
    S = LANES
    H, d = ML_HEADS, ML_DIM

    @pl.when(pl.program_id(1) == 0)
    def _():
        if has_init:
            c_ref[...] = c0_ref[...]
            n_ref[...] = n0_ref[...]
            m_ref[...] = m0_ref[...]
        else:
            c_ref[...] = jnp.zeros(c_ref.shape, F32)
            n_ref[...] = jnp.zeros(n_ref.shape, F32)
            m_ref[...] = jnp.full(m_ref.shape, M_EMPTY, F32)

    def pad(a):
        if L == S:
            return a
        return jnp.concatenate([a, jnp.zeros((S - L, a.shape[1]), a.dtype)], axis=0)

    gates = pad(gt_ref[...] + bg_ref[...])
    row = lax.broadcasted_iota(jnp.int32, (S, S), 0)
    col = lax.broadcasted_iota(jnp.int32, (S, S), 1)
    causal = col <= row
    valid = lax.broadcasted_iota(jnp.int32, (S, 1), 0) < L
    b_all = jnp.dot(causal.astype(F32), _log_sigmoid(gates), precision=lax.Precision.HIGHEST,
                    preferred_element_type=F32)
    r_t = (gates - pltpu.roll(b_all, LANES - H, axis=1)).T

    for h in range(H):
        hs = slice(h * d, (h + 1) * d)
        i_col = gates[:, h:h + 1]
        b_col = b_all[:, H + h:H + h + 1]
        g_col = b_col + m_ref[h]
        logd = b_col + r_t[h:h + 1, :]
        mt = jnp.maximum(g_col, jnp.max(jnp.where(causal, logd, -jnp.inf), axis=-1, keepdims=True))

        q = pad(q_ref[:, hs])
        k = pad(k_ref[:, hs]) * (d ** -0.5)
        v = pad(v_ref[:, hs])
        qb, kb = q.astype(BF16), k.astype(BF16)
        s = lax.dot_general(q, k, NT_DIMS, precision=lax.Precision.HIGHEST, preferred_element_type=F32)
        s = jnp.where(causal, s * jnp.exp(logd - mt), 0.0)
        gw = jnp.exp(g_col - mt)
        C = c_ref[0, h]
        n_row = n_ref[h]
        num = (jnp.dot(s.astype(BF16), v.astype(BF16), preferred_element_type=F32)
               + gw * jnp.dot(qb, C.astype(BF16), preferred_element_type=F32))
        den = jnp.sum(s, axis=-1, keepdims=True) + gw * jnp.sum(q * n_row, axis=-1, keepdims=True)
        den = jnp.maximum(jnp.abs(den), jnp.exp(-mt))
        hcell = num / den

        m_new = mt[L - 1:L, :]
        wk = jnp.where(valid, jnp.exp(b_col[L - 1:L, :] - b_col + i_col - m_new), 0.0)
        decay = jnp.exp(g_col[L - 1:L, :] - m_new)
        c_ref[0, h] = decay * C + lax.dot_general(kb, (wk * v).astype(BF16), TN_DIMS,
                                                  preferred_element_type=F32)
        n_ref[h] = decay * n_row + jnp.sum(wk * k, axis=0, keepdims=True)
        m_ref[h] = m_new

        hm = jax.nn.sigmoid(o_ref[:, hs]) * hcell[:L]
        mix_ref[:, hs] = _rmsnorm(hm, gn_ref[:, hs]).astype(mix_ref.dtype)


def _mlstm(proj, gates, b_gates, g_heads, layer, B, T, L, init, mix_dtype):
    nc = T // L
    H, d = ML_HEADS, ML_DIM
    W = ML_WIDTH
    has_init = init is not None

    def qkvo(colblk):
        return pl.BlockSpec((L, W), lambda b, c: (b * nc + c, colblk))

    state_specs = [pl.BlockSpec((1, H, d, d), lambda b, c: (b, 0, 0, 0)),
                   pl.BlockSpec((H, 1, d), lambda b, c: (b, 0, 0)),
                   pl.BlockSpec((H, 1, 1), lambda b, c: (b, 0, 0))]
    in_specs = [qkvo(COL_MQ), qkvo(COL_MK), qkvo(COL_MV), qkvo(COL_MO),
                pl.BlockSpec((L, LANES), lambda b, c: (b * nc + c, 0)),
                pl.BlockSpec((None, 1, LANES), lambda b, c: (layer, 0, 0)),
                pl.BlockSpec((None, 1, W), lambda b, c: (layer, 0, 0))]
    args = [proj, proj, proj, proj, gates, b_gates, g_heads]
    if has_init:
        c0, n0, m0 = init
        depth = c0.shape[0]
        in_specs += [pl.BlockSpec((None, 1, H, d, d), lambda b, c: (layer, b, 0, 0, 0)),
                     pl.BlockSpec((None, H, 1, d), lambda b, c: (layer, b, 0, 0)),
                     pl.BlockSpec((None, H, 1, 1), lambda b, c: (layer, b, 0, 0))]
        args += [c0, n0.reshape(depth, B * H, 1, d), m0.reshape(depth, B * H, 1, 1)]
    mix, c, n, m = pl.pallas_call(
        functools.partial(_mlstm_kernel, L=L, has_init=has_init),
        grid=(B, nc),
        in_specs=in_specs,
        out_specs=[pl.BlockSpec((L, W), lambda b, c: (b * nc + c, 0))] + state_specs,
        out_shape=[jax.ShapeDtypeStruct((B * T, W), mix_dtype),
                   jax.ShapeDtypeStruct((B, H, d, d), F32),
                   jax.ShapeDtypeStruct((B * H, 1, d), F32),
                   jax.ShapeDtypeStruct((B * H, 1, 1), F32)],
        compiler_params=_params(("parallel", "arbitrary")),
        name="mlstm",
    )(*args)
    return mix, c, n.reshape(B, H, d), m.reshape(B, H)


def _sb_block(z, u, carry, mask):
    sp = _softplus(z)
    lk = -sp
    if mask is not None:
        lk = jnp.where(mask, lk, 0.0)
    after = _split_dot(lk, u)
    a = jnp.exp((z - sp) + after + carry)
    if mask is not None:
        a = jnp.where(mask, a, 0.0)
    return a, carry + (after[:, 0:1] + lk[:, 0:1])


def _sb_prompt_kernel(bias_ref, ti_ref, tj_ref, q_ref, k_ref, v_ref, gn_ref, o_ref,
                      qb_ref, kb_ref, vb_ref, madd_ref, ls_ref, lk_ref, e_ref, acc_ref, carry_ref, *,
                      TQ, n_tiles, layer):
    h = pl.program_id(1)
    bias = bias_ref[layer, h]
    scale = SB_DIM ** -0.5
    row = lax.broadcasted_iota(jnp.int32, (TQ, TQ), 0)
    col = lax.broadcasted_iota(jnp.int32, (TQ, TQ), 1)
    u = (row > col).astype(BF16)

    qb_ref[...] = q_ref[...].astype(BF16)
    kb_ref[...] = k_ref[...].astype(BF16)
    vb_ref[...] = v_ref[...].astype(BF16)
    madd_ref[0] = jnp.zeros((TQ, TQ), F32)
    madd_ref[1] = jnp.where(col < row, 0.0, M_EMPTY)
    ls_ref[...] = jnp.full((TQ, TQ), M_EMPTY, F32)
    lk_ref[...] = jnp.zeros((TQ, TQ), F32)
    e_ref[...] = jnp.full((TQ, TQ), M_EMPTY, F32)
    acc_ref[...] = jnp.zeros(acc_ref.shape, F32)
    carry_ref[...] = jnp.zeros(carry_ref.shape, F32)

    def rows(blk):
        return pl.ds(pl.multiple_of(blk * TQ, TQ), TQ)

    def body(t, tot):
        ia, ja = ti_ref[t], tj_ref[t]
        z = lax.dot_general(qb_ref[rows(ia), :], kb_ref[rows(ja), :], NT_DIMS, preferred_element_type=F32)
        z = (z * scale + bias) + madd_ref[(ia == ja).astype(jnp.int32)]
        tc = jnp.maximum(t - 2, 0)
        ic, jc = ti_ref[tc], tj_ref[tc]
        a = jnp.exp(e_ref[...] + carry_ref[...])
        acc_ref[...] += jnp.dot(a.astype(BF16), vb_ref[rows(jc), :], preferred_element_type=F32)
        carry_ref[...] += tot
        lk = lk_ref[...]
        after = _split_dot(lk, u)
        e_ref[...] = ls_ref[...] + after
        tot = after[:, 0:1] + lk[:, 0:1]
        sp = _softplus(z)
        ls_ref[...] = z - sp
        lk_ref[...] = -sp

        @pl.when(jnp.logical_and(t >= 2, jc == 0))
        def _():
            o_ref[rows(ic), :] = _rmsnorm(acc_ref[...], gn_ref[...]).astype(o_ref.dtype)
            acc_ref[...] = jnp.zeros(acc_ref.shape, F32)
            carry_ref[...] = jnp.zeros(carry_ref.shape, F32)

        return tot

    lax.fori_loop(0, n_tiles + 2, body, jnp.zeros((TQ, 1), F32))


def _sb_prompt(proj, b_sb, g_heads, layer, B, T):
    TQ = min(256, T)
    nq = T // TQ
    H, d = SB_HEADS, SB_DIM
    tiles = [(i, j) for i in range(nq) for j in range(i, -1, -1)]
    tiles += [tiles[-1]] * 2
    ti = jnp.asarray([t[0] for t in tiles], jnp.int32)
    tj = jnp.asarray([t[1] for t in tiles], jnp.int32)
    smem = pl.BlockSpec(memory_space=pltpu.SMEM)
    return pl.pallas_call(
        functools.partial(_sb_prompt_kernel, TQ=TQ, n_tiles=len(tiles) - 2, layer=layer),
        grid=(B, H),
        in_specs=[smem, smem, smem,
                  pl.BlockSpec((T, d), lambda b, h: (b, COL_SQ * H + h)),
                  pl.BlockSpec((T, d), lambda b, h: (b, COL_SK * H + h)),
                  pl.BlockSpec((T, d), lambda b, h: (b, COL_SV * H + h)),
                  pl.BlockSpec((None, 1, d), lambda b, h: (layer, 0, ML_WIDTH // d + h))],
        out_specs=pl.BlockSpec((T, d), lambda b, h: (b, h)),
        out_shape=jax.ShapeDtypeStruct((B * T, SB_WIDTH), BF16),
        scratch_shapes=[pltpu.VMEM((T, d), BF16), pltpu.VMEM((T, d), BF16), pltpu.VMEM((T, d), BF16),
                        pltpu.VMEM((2, TQ, TQ), F32),
                        pltpu.VMEM((TQ, TQ), F32), pltpu.VMEM((TQ, TQ), F32), pltpu.VMEM((TQ, TQ), F32),
                        pltpu.VMEM((TQ, d), F32), pltpu.VMEM((TQ, 1), F32)],
        compiler_params=_params(("parallel", "parallel")),
        name="sb_prompt",
    )(b_sb, ti, tj, proj, proj, proj, g_heads)


def _sb_sample_kernel(pt_ref, q_ref, kn_ref, vn_ref, bias_ref, gn_ref, *rest, T, G):
    k_refs, v_refs = rest[:G], rest[G:2 * G]
    o_ref, acc_ref, carry_ref = rest[2 * G:]
    H, d = SB_HEADS, SB_DIM
    R = H * T
    s_id = pl.program_id(1)
    scale = d ** -0.5
    row = lax.broadcasted_iota(jnp.int32, (PAGE, PAGE), 0)
    col = lax.broadcasted_iota(jnp.int32, (PAGE, PAGE), 1)
    u = (row > col).astype(BF16)
    qb = q_ref[...].astype(BF16)
    bias = bias_ref[...]

    def logits(keys_of_head):
        z = [lax.dot_general(qb[h * T:(h + 1) * T], keys_of_head(h), NT_DIMS, preferred_element_type=F32)
             for h in range(H)]
        return jnp.concatenate(z, axis=0) * scale + bias

    def weighted_values(a, values_of_head):
        ab = a.astype(BF16)
        o = [jnp.dot(ab[h * T:(h + 1) * T], values_of_head(h), preferred_element_type=F32) for h in range(H)]
        return jnp.concatenate(o, axis=0)

    @pl.when(s_id == 0)
    def _():
        zpad = jnp.zeros((PAGE - T, d), F32)
        new_rows = lambda ref: lambda h: jnp.concatenate([ref[h * T:(h + 1) * T, :], zpad], axis=0).astype(BF16)
        z = logits(new_rows(kn_ref))
        mask = (lax.broadcasted_iota(jnp.int32, (R, PAGE), 1) < lax.broadcasted_iota(jnp.int32, (R, PAGE), 0) % T)
        a, carry = _sb_block(z, u, jnp.zeros((R, 1), F32), mask)
        carry_ref[...] = carry
        acc_ref[...] = weighted_values(a, new_rows(vn_ref))

    def page_rows(refs):
        return lambda h: jnp.concatenate([r[pl.ds(h, PAGE, stride=H), :] for r in refs], axis=0).astype(BF16)

    z = logits(page_rows(k_refs))
    sp = _softplus(z)
    lk = -sp
    lk_st = jnp.concatenate([lk[:, g * PAGE:(g + 1) * PAGE] for g in range(G)], axis=0)
    after_st = _split_dot(lk_st, u)
    tot_st = after_st[:, 0:1] + lk_st[:, 0:1]
    carry = carry_ref[...]
    after = []
    for g in range(G):
        after.append(after_st[g * R:(g + 1) * R] + carry)
        carry = carry + tot_st[g * R:(g + 1) * R]
    carry_ref[...] = carry
    a = jnp.exp((z - sp) + jnp.concatenate(after, axis=1))
    acc_ref[...] += weighted_values(a, page_rows(v_refs))

    @pl.when(s_id == pl.num_programs(1) - 1)
    def _():
        o_ref[...] = _rmsnorm(acc_ref[...], gn_ref[...])


def _sb_sample(q, k_new, v_new, cache_k, cache_v, layer, page_table, bias_col, gn_rows, B, T):
    n_pages = page_table.shape[1]
    G = 8 if n_pages % 8 == 0 else 1
    nsteps = n_pages // G
    H, d = SB_HEADS, SB_DIM
    R = H * T

    def page_spec(g):
        return pl.BlockSpec((None, None, PAGE * H, d),
                            lambda b, s, pt: (layer, pt[b, n_pages - 1 - (s * G + g)], 0, 0))

    row_spec = pl.BlockSpec((None, R, d), lambda b, s, pt: (b, 0, 0))
    grid_spec = pltpu.PrefetchScalarGridSpec(
        num_scalar_prefetch=1,
        grid=(B, nsteps),
        in_specs=[row_spec, row_spec, row_spec,
                  pl.BlockSpec((None, R, 1), lambda b, s, pt: (layer, 0, 0)),
                  pl.BlockSpec((None, R, d), lambda b, s, pt: (layer, 0, 0))]
                 + [page_spec(g) for g in range(G)] + [page_spec(g) for g in range(G)],
        out_specs=row_spec,
        scratch_shapes=[pltpu.VMEM((R, d), F32), pltpu.VMEM((R, 1), F32)],
    )
    return pl.pallas_call(
        functools.partial(_sb_sample_kernel, T=T, G=G),
        grid_spec=grid_spec,
        out_shape=jax.ShapeDtypeStruct((B, R, d), F32),
        compiler_params=_params(("parallel", "arbitrary")),
        name="sb_sample",
    )(page_table, q, k_new, v_new, bias_col, gn_rows, *([cache_k] * G), *([cache_v] * G))


def _out_proj_kernel(a1_ref, a2_ref, w1_ref, w2_ref, x_ref, o_ref):
    o_ref[...] = (x_ref[...]
                  + jnp.dot(a1_ref[...].astype(BF16), w1_ref[...], preferred_element_type=F32)
                  + jnp.dot(a2_ref[...].astype(BF16), w2_ref[...], preferred_element_type=F32))


def _out_proj(a1, a2, w, layer, x, tm):
    M, D = x.shape
    K1, K2 = a1.shape[1], a2.shape[1]
    return pl.pallas_call(
        _out_proj_kernel,
        grid=(M // tm,),
        in_specs=[pl.BlockSpec((tm, K1), lambda i: (i, 0)),
                  pl.BlockSpec((tm, K2), lambda i: (i, 0)),
                  pl.BlockSpec((None, K1, D), lambda i: (layer, 0, 0)),
                  pl.BlockSpec((None, K2, D), lambda i: (layer, K1 // K2, 0)),
                  pl.BlockSpec((tm, D), lambda i: (i, 0))],
        out_specs=pl.BlockSpec((tm, D), lambda i: (i, 0)),
        out_shape=jax.ShapeDtypeStruct((M, D), F32),
        compiler_params=_params(("parallel",)),
        name="out_proj",
    )(a1, a2, w, w, x)


def _cross_kernel(x_ref, g_ref, wq_ref, mk_ref, mv_ref, wo_ref, o_ref):
    x = x_ref[...]
    hn = _rmsnorm(x, g_ref[...]).astype(BF16)
    q = jnp.dot(hn, wq_ref[...], preferred_element_type=F32)
    scale = X_DIM ** -0.5
    outs = []
    for h in range(X_HEADS):
        sl = slice(h * X_DIM, (h + 1) * X_DIM)
        s = lax.dot_general(q[:, sl].astype(BF16), mk_ref[:, sl].astype(BF16), NT_DIMS,
                            preferred_element_type=F32) * scale
        e = jnp.exp(s - jnp.max(s, axis=-1, keepdims=True))
        p = e / jnp.sum(e, axis=-1, keepdims=True)
        outs.append(jnp.dot(p.astype(BF16), mv_ref[:, sl].astype(BF16), preferred_element_type=F32))
    o = jnp.concatenate(outs, axis=1).astype(BF16)
    o_ref[...] = x + jnp.dot(o, wo_ref[...], preferred_element_type=F32)


def _cross(x, g, wq, wo, layer, mk, mv, mk_spec, mv_spec, tm):
    M, D = x.shape
    return pl.pallas_call(
        _cross_kernel,
        grid=(M // tm,),
        in_specs=[pl.BlockSpec((tm, D), lambda i: (i, 0)),
                  pl.BlockSpec((None, 1, D), lambda i: (layer, 0, 0)),
                  pl.BlockSpec((None, D, X_WIDTH), lambda i: (layer, 0, 0)),
                  mk_spec, mv_spec,
                  pl.BlockSpec((None, X_WIDTH, D), lambda i: (layer, 0, 0))],
        out_specs=pl.BlockSpec((tm, D), lambda i: (i, 0)),
        out_shape=jax.ShapeDtypeStruct((M, D), F32),
        compiler_params=_params(("parallel",)),
        name="cross",
    )(x, g, wq, mk, mv, wo)


def _ffn_kernel(*refs, tm, from_buf, tiles_per_seq):
    if from_buf:
        x_ref, g_ref, wa_ref, wb_ref, cw_ref, cb_ref, wd_ref, buf_ref, o_ref, st_ref, hn_ref, acc_ref = refs
    else:
        x_ref, xh_ref, g_ref, wa_ref, wb_ref, cw_ref, cb_ref, wd_ref, o_ref, st_ref, hn_ref, acc_ref = refs
    HALO = SUBLANES
    f = pl.program_id(1)

    @pl.when(f == 0)
    def _():
        hn_ref[HALO:, :] = _rmsnorm(x_ref[...], g_ref[...]).astype(BF16)
        if not from_buf:
            hn_ref[:HALO, :] = _rmsnorm(xh_ref[...], g_ref[...]).astype(BF16)
        acc_ref[...] = jnp.zeros(acc_ref.shape, F32)

    tf = wa_ref.shape[1]
    if from_buf:
        a = jnp.dot(hn_ref[HALO:, :], wa_ref[...], preferred_element_type=F32)
        nb = buf_ref.shape[1]
        a_ext = jnp.concatenate([buf_ref[j] for j in range(CONV_W - 1)] + [a], axis=0)
        taps = [a_ext[j * nb:j * nb + tm, :] for j in range(CONV_W)]
    else:
        a_ext = jnp.dot(hn_ref[...], wa_ref[...], preferred_element_type=F32)
        seq_start = (pl.program_id(0) % tiles_per_seq) == 0
        rows = lax.broadcasted_iota(jnp.int32, (HALO + tm, 1), 0)
        a_ext = jnp.where(jnp.logical_and(seq_start, rows < HALO), 0.0, a_ext)
        taps = [pltpu.roll(a_ext, 2, axis=0)[HALO:, :], pltpu.roll(a_ext, 1, axis=0)[HALO:, :], a_ext[HALO:, :]]
    b = jnp.dot(hn_ref[HALO:, :], wb_ref[...], preferred_element_type=F32)
    c = cb_ref[...] + ((taps[0] * cw_ref[0:1, :] + taps[1] * cw_ref[1:2, :]) + taps[2] * cw_ref[2:3, :])
    gate = (c * jax.nn.sigmoid(c)) * b
    acc_ref[...] += jnp.dot(gate.astype(BF16), wd_ref[...], preferred_element_type=F32)
    st_ref[...] = taps[2][tm - st_ref.shape[0]:, :]

    @pl.when(f == pl.num_programs(1) - 1)
    def _():
        o_ref[...] = x_ref[...] + acc_ref[...]


def _ffn(x, g, w_up, conv_w, conv_b, w_down, layer, B, T, tm, tf, buf=None):
    M, D = x.shape
    F = w_down.shape[1]
    nf = F // tf
    from_buf = buf is not None
    assert F % tf == 0 and tm % SUBLANES == 0 and (tm == M and B % SUBLANES == 0 if from_buf else T % tm == 0)
    tiles_per_seq = 1 if from_buf else T // tm
    st_rows = (CONV_W - 1) * B if from_buf else SUBLANES
    hb = tm // SUBLANES
    in_specs = [pl.BlockSpec((tm, D), lambda i, f: (i, 0))]
    args = [x]
    if not from_buf:
        in_specs.append(pl.BlockSpec((SUBLANES, D), lambda i, f: (jnp.maximum(i * hb - 1, 0), 0)))
        args.append(x)
    in_specs += [pl.BlockSpec((None, 1, D), lambda i, f: (layer, 0, 0)),
                 pl.BlockSpec((None, D, tf), lambda i, f: (layer, 0, f)),
                 pl.BlockSpec((None, D, tf), lambda i, f: (layer, 0, nf + f)),
                 pl.BlockSpec((None, CONV_W, tf), lambda i, f: (layer, 0, f)),
                 pl.BlockSpec((None, 1, tf), lambda i, f: (layer, 0, f)),
                 pl.BlockSpec((None, tf, D), lambda i, f: (layer, f, 0))]
    args += [g, w_up, w_up, conv_w, conv_b, w_down]
    if from_buf:
        in_specs.append(pl.BlockSpec((None, CONV_W - 1, B, tf), lambda i, f: (layer, 0, 0, f)))
        args.append(buf)
    y, st = pl.pallas_call(
        functools.partial(_ffn_kernel, tm=tm, from_buf=from_buf, tiles_per_seq=tiles_per_seq),
        grid=(M // tm, nf),
        in_specs=in_specs,
        out_specs=[pl.BlockSpec((tm, D), lambda i, f: (i, 0)),
                   pl.BlockSpec((None, st_rows, tf), lambda i, f: (i, 0, f))],
        out_shape=[jax.ShapeDtypeStruct((M, D), F32),
                   jax.ShapeDtypeStruct((M // tm, st_rows, F), F32)],
        scratch_shapes=[pltpu.VMEM((SUBLANES + tm, D), BF16), pltpu.VMEM((tm, D), F32)],
        compiler_params=_params(("parallel", "arbitrary")),
        name="ffn",
    )(*args)
    if from_buf:
        return y, st.reshape(CONV_W - 1, B, F).transpose(1, 0, 2)
    return y, st[tiles_per_seq - 1::tiles_per_seq, SUBLANES - (CONV_W - 1):, :]


def _norm_kernel(x_ref, g_ref, o_ref):
    o_ref[...] = _rmsnorm(x_ref[...], g_ref[...])


def _final_norm(x, g, tm):
    M, D = x.shape
    return pl.pallas_call(
        _norm_kernel,
        grid=(M // tm,),
        in_specs=[pl.BlockSpec((tm, D), lambda i: (i, 0)), pl.BlockSpec((1, D), lambda i: (0, 0))],
        out_specs=pl.BlockSpec((tm, D), lambda i: (i, 0)),
        out_shape=jax.ShapeDtypeStruct((M, D), F32),
        compiler_params=_params(("parallel",)),
        name="final_norm",
    )(x, g.reshape(1, D))


def _prep_params(T_s, w_in, b_gates, b_sb, norm_mix, norm_heads, w_out, norm_cross, norm_mem, wx_q, wx_kv, wx_o,
                 norm_ffn, w_up, conv_w, conv_b, w_down):
    depth, D, _ = w_in.shape
    H, d = SB_HEADS, SB_DIM
    g0 = 4 * ML_WIDTH
    g1 = g0 + 2 * ML_HEADS
    pad = LANES - 2 * ML_HEADS
    row = lambda a: a.reshape(depth, 1, -1)
    return dict(
        w_ml=w_in[:, :, :g0].astype(BF16), w_sb=w_in[:, :, g1:].astype(BF16),
        w_gate=jnp.concatenate([w_in[:, :, g0:g1], jnp.zeros((depth, D, pad), F32)], axis=2).astype(BF16),
        bg=row(jnp.concatenate([b_gates, jnp.zeros((depth, pad), F32)], axis=1)),
        w_out=w_out.astype(BF16), wx_q=wx_q.astype(BF16), wx_kv=wx_kv.astype(BF16), wx_o=wx_o.astype(BF16),
        w_up=w_up.astype(BF16), w_down=w_down.astype(BF16),
        norm_mix=row(norm_mix), norm_heads=row(norm_heads), norm_cross=row(norm_cross), norm_mem=row(norm_mem),
        norm_ffn=row(norm_ffn), conv_w=conv_w, conv_b=row(conv_b), b_sb=b_sb,
        bias_col=jnp.repeat(b_sb, T_s, axis=1).reshape(depth, H * T_s, 1),
        gn_rows=jnp.repeat(norm_heads[:, ML_WIDTH:].reshape(depth, H, d), T_s, axis=1))


def _mix_inputs(x, layer, p, tm):
    return _in_proj(x, p["norm_mix"], p["w_ml"], p["w_sb"], p["w_gate"], layer, tm, 1024)


def _prompt_layer(x, mem, B, T, layer, p):
    n_mem = mem.shape[0] // B
    kv = _norm_matmul(mem, p["norm_mem"], p["wx_kv"], layer, min(512, mem.shape[0]), 512)
    tm = min(512, T)
    proj, gates = _mix_inputs(x, layer, p, min(1024, T))
    mix_ml, c, n, m = _mlstm(proj, gates, p["bg"], p["norm_heads"], layer, B, T, min(LANES, T), None, BF16)
    mix_sb = _sb_prompt(proj, p["b_sb"], p["norm_heads"], layer, B, T)
    x = _out_proj(mix_ml, mix_sb, p["w_out"], layer, x, tm)
    tpb = T // tm
    x = _cross(x, p["norm_cross"], p["wx_q"], p["wx_o"], layer, kv, kv,
               pl.BlockSpec((n_mem, X_WIDTH), lambda i: (i // tpb, 0)),
               pl.BlockSpec((n_mem, X_WIDTH), lambda i: (i // tpb, 1)), tm)
    x, st = _ffn(x, p["norm_ffn"], p["w_up"], p["conv_w"], p["conv_b"], p["w_down"], layer, B, T, tm, 512)
    return x, (proj, c, n, m, st, kv)


def _sample_layer(x, B, T, layer, p, cache_k, cache_v, page_table, state, conv_state, mem_k, mem_v):
    M = B * T
    H, d = SB_HEADS, SB_DIM
    proj, gates = _mix_inputs(x, layer, p, M)
    mix_ml, c, n, m = _mlstm(proj, gates, p["bg"], p["norm_heads"], layer, B, T, T, state, F32)
    heads = lambda colblk: proj[:, colblk * SB_WIDTH:(colblk + 1) * SB_WIDTH].reshape(B, T, H, d)
    sbk, sbv = heads(COL_SK), heads(COL_SV)
    head_rows = lambda a: a.transpose(0, 2, 1, 3).reshape(B, H * T, d)
    mix_sb = _sb_sample(head_rows(heads(COL_SQ)), head_rows(sbk), head_rows(sbv), cache_k, cache_v, layer,
                        page_table, p["bias_col"], p["gn_rows"], B, T)
    mix_sb = mix_sb.reshape(B, H, T, d).transpose(0, 2, 1, 3).reshape(M, SB_WIDTH)
    x = _out_proj(mix_ml, mix_sb, p["w_out"], layer, x, M)
    n_mem = mem_k.shape[2]
    mem_spec = pl.BlockSpec((None, None, n_mem, X_WIDTH), lambda i: (layer, i, 0, 0))
    x = _cross(x, p["norm_cross"], p["wx_q"], p["wx_o"], layer, mem_k, mem_v, mem_spec, mem_spec, T)
    time_major = lambda a, n0, n1: a.reshape(n0, n1, -1).transpose(1, 0, 2).reshape(M, -1)
    x, st = _ffn(time_major(x, B, T), p["norm_ffn"], p["w_up"], p["conv_w"], p["conv_b"], p["w_down"], layer,
                 B, T, M, 512, buf=conv_state)
    x = time_major(x, T, B)
    return x, (sbk, sbv, c, n, m, st)


def kernel(x_prompt, x_sample, mem_prompt, cache_sb_k, cache_sb_v, state_mlstm_c, state_mlstm_n, state_mlstm_m,
           state_ffn_conv, cache_mem_k, cache_mem_v, page_table, norm_mix, w_in, b_gates, b_sb, norm_heads, w_out,
           norm_cross, norm_mem, wx_q, wx_kv, wx_o, norm_ffn, w_up, conv_w, conv_b, w_down, norm_final):
    Bp, Tp, D = x_prompt.shape
    Bs, Ts, _ = x_sample.shape
    depth = w_in.shape[0]
    n_mem = cache_mem_k.shape[2]
    n_pool = cache_sb_k.shape[1]
    p = _prep_params(Ts, w_in, b_gates, b_sb, norm_mix, norm_heads, w_out, norm_cross, norm_mem, wx_q, wx_kv, wx_o,
                     norm_ffn, w_up, conv_w, conv_b, w_down)
    state = (state_mlstm_c, state_mlstm_n, state_mlstm_m)
    conv_state = state_ffn_conv.transpose(0, 2, 1, 3)
    cache_k = cache_sb_k.reshape(depth, n_pool, PAGE * SB_HEADS, SB_DIM)
    cache_v = cache_sb_v.reshape(depth, n_pool, PAGE * SB_HEADS, SB_DIM)
    mem_k = cache_mem_k.reshape(depth, Bs, n_mem, X_WIDTH)
    mem_v = cache_mem_v.reshape(depth, Bs, n_mem, X_WIDTH)
    yp = x_prompt.reshape(Bp * Tp, D)
    ys = x_sample.reshape(Bs * Ts, D)
    mem = mem_prompt.reshape(Bp * mem_prompt.shape[1], D)
    P = [[] for _ in range(6)]
    S = [[] for _ in range(6)]
    for l in range(depth):
        yp, st_p = _prompt_layer(yp, mem, Bp, Tp, l, p)
        for lst, a in zip(P, st_p):
            lst.append(a)
        ys, st_s = _sample_layer(ys, Bs, Ts, l, p, cache_k, cache_v, page_table, state, conv_state, mem_k, mem_v)
        for lst, a in zip(S, st_s):
            lst.append(a)
    y_prompt = _final_norm(yp, norm_final, min(512, Bp * Tp)).reshape(Bp, Tp, D)
    y_sample = _final_norm(ys, norm_final, Bs * Ts).reshape(Bs, Ts, D)
    projs, pc, pn, pm, pconv, kvs = P
    cols = lambda a, j: jnp.stack([x[:, j * SB_WIDTH:(j + 1) * SB_WIDTH] for x in a])
    p_sb_k = cols(projs, COL_SK).reshape(depth, Bp, Tp, SB_HEADS, SB_DIM)
    p_sb_v = cols(projs, COL_SV).reshape(depth, Bp, Tp, SB_HEADS, SB_DIM)
    kv = jnp.stack(kvs)
    p_mem_k = kv[:, :, :X_WIDTH].reshape(depth, Bp, n_mem, X_HEADS, X_DIM)
    p_mem_v = kv[:, :, X_WIDTH:].reshape(depth, Bp, n_mem, X_HEADS, X_DIM)
    return ((y_prompt, y_sample, p_sb_k, p_sb_v, jnp.stack(pc), jnp.stack(pn), jnp.stack(pm), jnp.stack(pconv),
             p_mem_k, p_mem_v) + tuple(jnp.stack(a) for a in S))
```

```python
import functools

import jax
import jax.numpy as jnp
from jax import lax
from jax.experimental import pallas as pl
from jax.experimental.pallas import tpu as pltpu

F32 = jnp.float32
BF16 = jnp.bfloat16

EPS = 1e-6
M_EMPTY = -1e30
ML_HEADS = 4
ML_DIM = 256
SB_HEADS = 8
SB_DIM = 128
X_HEADS = 4
X_DIM = 128
PAGE = 128
CONV_W = 3
ML_WIDTH = ML_HEADS * ML_DIM
SB_WIDTH = SB_HEADS * SB_DIM
X_WIDTH = X_HEADS * X_DIM

SUBLANES = 8
LANES = 128
MIB = 1024 * 1024
VMEM_LIMIT = 56 * MIB

COL_MQ, COL_MK, COL_MV, COL_MO = 0, 1, 2, 3
COL_SQ, COL_SK, COL_SV = 4, 5, 6
N_PROJ = 7 * ML_WIDTH

NT_DIMS = (((1,), (1,)), ((), ()))
TN_DIMS = (((0,), (0,)), ((), ()))


def _params(sem):
    return pltpu.CompilerParams(dimension_semantics=sem, vmem_limit_bytes=VMEM_LIMIT)


def _rmsnorm(xf, g):
    return xf * lax.rsqrt(jnp.mean(xf * xf, axis=-1, keepdims=True) + EPS) * g


def _log_sigmoid(x):
    return jnp.minimum(x, 0.0) - jnp.log1p(jnp.exp(-jnp.abs(x)))


def _softplus(z):
    t = jnp.exp(-jnp.abs(z))
    u = 1.0 + t
    return jnp.maximum(z, 0.0) + (jnp.log(u) + (t - (u - 1.0)))


def _split_dot(x, u):
    hi = x.astype(BF16)
    lo = (x - hi.astype(F32)).astype(BF16)
    return (jnp.dot(hi, u, preferred_element_type=F32) + jnp.dot(lo, u, preferred_element_type=F32))


def _norm_matmul_kernel(x_ref, g_ref, w_ref, o_ref, hn_ref):
    @pl.when(pl.program_id(1) == 0)
    def _():
        hn_ref[...] = _rmsnorm(x_ref[...], g_ref[...]).astype(BF16)

    o_ref[...] = jnp.dot(hn_ref[...], w_ref[...], preferred_element_type=F32)


def _norm_matmul(x, g, w, layer, tm, tn):
    M, D = x.shape
    N = w.shape[2]
    assert M % tm == 0 and N % tn == 0
    return pl.pallas_call(
        _norm_matmul_kernel,
        grid=(M // tm, N // tn),
        in_specs=[pl.BlockSpec((tm, D), lambda i, j: (i, 0)),
                  pl.BlockSpec((None, 1, D), lambda i, j: (layer, 0, 0)),
                  pl.BlockSpec((None, D, tn), lambda i, j: (layer, 0, j))],
        out_specs=pl.BlockSpec((tm, tn), lambda i, j: (i, j)),
        out_shape=jax.ShapeDtypeStruct((M, N), F32),
        scratch_shapes=[pltpu.VMEM((tm, D), BF16)],
        compiler_params=_params(("parallel", "arbitrary")),
        name="norm_matmul",
    )(x, g, w)


def _in_proj_kernel(x_ref, g_ref, wml_ref, wsb_ref, wg_ref, o_ref, og_ref, hn_ref, *, n_ml):
    n = pl.program_id(1)

    @pl.when(n == 0)
    def _():
        hn_ref[...] = _rmsnorm(x_ref[...], g_ref[...]).astype(BF16)
        og_ref[...] = jnp.dot(hn_ref[...], wg_ref[...], preferred_element_type=F32)

    @pl.when(n < n_ml)
    def _():
        o_ref[...] = jnp.dot(hn_ref[...], wml_ref[...], preferred_element_type=F32)

    @pl.when(n >= n_ml)
    def _():
        o_ref[...] = jnp.dot(hn_ref[...], wsb_ref[...], preferred_element_type=F32)


def _in_proj(x, g, w_ml, w_sb, w_gate, layer, tm, tn):
    M, D = x.shape
    n_ml, n_sb = w_ml.shape[2] // tn, w_sb.shape[2] // tn
    assert M % tm == 0 and w_ml.shape[2] % tn == 0 and w_sb.shape[2] % tn == 0
    return pl.pallas_call(
        functools.partial(_in_proj_kernel, n_ml=n_ml),
        grid=(M // tm, n_ml + n_sb),
        in_specs=[pl.BlockSpec((tm, D), lambda i, j: (i, 0)),
                  pl.BlockSpec((None, 1, D), lambda i, j: (layer, 0, 0)),
                  pl.BlockSpec((None, D, tn), lambda i, j: (layer, 0, jnp.minimum(j, n_ml - 1))),
                  pl.BlockSpec((None, D, tn), lambda i, j: (layer, 0, jnp.maximum(j - n_ml, 0))),
                  pl.BlockSpec((None, D, LANES), lambda i, j: (layer, 0, 0))],
        out_specs=[pl.BlockSpec((tm, tn), lambda i, j: (i, j)),
                   pl.BlockSpec((tm, LANES), lambda i, j: (i, 0))],
        out_shape=[jax.ShapeDtypeStruct((M, (n_ml + n_sb) * tn), F32),
                   jax.ShapeDtypeStruct((M, LANES), F32)],
        scratch_shapes=[pltpu.VMEM((tm, D), BF16)],
        compiler_params=_params(("parallel", "arbitrary")),
        name="in_proj",
    )(x, g, w_ml, w_sb, w_gate)


def _mlstm_kernel(q_ref, k_ref, v_ref, o_ref, gt_ref, bg_ref, gn_ref, *rest, L, has_init):
    if has_init:
        c0_ref, n0_ref, m0_ref, mix_ref, c_ref, n_ref, m_ref = rest
    else:
        mix_ref, c_ref, n_ref, m_ref = rest
    S = LANES
    H, d = ML_HEADS, ML_DIM

    @pl.when(pl.program_id(1) == 0)
    def _():
        if has_init:
            c_ref[...] = c0_ref[...]
            n_ref[...] = n0_ref[...]
            m_ref[...] = m0_ref[...]
        else:
            c_ref[...] = jnp.zeros(c_ref.shape, F32)
            n_ref[...] = jnp.zeros(n_ref.shape, F32)
            m_ref[...] = jnp.full(m_ref.shape, M_EMPTY, F32)

    def pad(a):
        if L == S:
            return a
        return jnp.concatenate([a, jnp.zeros((S - L, a.shape[1]), a.dtype)], axis=0)

    gates = pad(gt_ref[...] + bg_ref[...])
    row = lax.broadcasted_iota(jnp.int32, (S, S), 0)
    col = lax.broadcasted_iota(jnp.int32, (S, S), 1)
    causal = col <= row
    valid = lax.broadcasted_iota(jnp.int32, (S, 1), 0) < L
    b_all = jnp.dot(causal.astype(F32), _log_sigmoid(gates), precision=lax.Precision.HIGHEST,
                    preferred_element_type=F32)
    r_t = (gates - pltpu.roll(b_all, LANES - H, axis=1)).T

    for h in range(H):
        hs = slice(h * d, (h + 1) * d)
        i_col = gates[:, h:h + 1]
        b_col = b_all[:, H + h:H + h + 1]
        g_col = b_col + m_ref[h]
        logd = b_col + r_t[h:h + 1, :]
        mt = jnp.maximum(g_col, jnp.max(jnp.where(causal, logd, -jnp.inf), axis=-1, keepdims=True))

        q = pad(q_ref[:, hs])
        k = pad(k_ref[:, hs]) * (d ** -0.5)
        v = pad(v_ref[:, hs])
        qb, kb = q.astype(BF16), k.astype(BF16)
        s = lax.dot_general(q, k, NT_DIMS, precision=lax.Precision.HIGHEST, preferred_element_type=F32)
        s = jnp.where(causal, s * jnp.exp(logd - mt), 0.0)
        gw = jnp.exp(g_col - mt)
        C = c_ref[0, h]
        n_row = n_ref[h]
        num = (jnp.dot(s.astype(BF16), v.astype(BF16), preferred_element_type=F32)
               + gw * jnp.dot(qb, C.astype(BF16), preferred_element_type=F32))
        den = jnp.sum(s, axis=-1, keepdims=True) + gw * jnp.sum(q * n_row, axis=-1, keepdims=True)
        den = jnp.maximum(jnp.abs(den), jnp.exp(-mt))
        hcell = num / den

        m_new = mt[L - 1:L, :]
        wk = jnp.where(valid, jnp.exp(b_col[L - 1:L, :] - b_col + i_col - m_new), 0.0)
        decay = jnp.exp(g_col[L - 1:L, :] - m_new)
        c_ref[0, h] = decay * C + lax.dot_general(kb, (wk * v).astype(BF16), TN_DIMS,
                                                  preferred_element_type=F32)
        n_ref[h] = decay * n_row + jnp.sum(wk * k, axis=0, keepdims=True)
        m_ref[h] = m_new

        hm = jax.nn.sigmoid(o_ref[:, hs]) * hcell[:L]
        mix_ref[:, hs] = _rmsnorm(hm, gn_ref[:, hs]).astype(mix_ref.dtype)


def _mlstm(proj, gates, b_gates, g_heads, layer, B, T, L, init, mix_dtype):
    nc = T // L
    H, d = ML_HEADS, ML_DIM
    W = ML_WIDTH
    has_init = init is not None

    def qkvo(colblk):
        return pl.BlockSpec((L, W), lambda b, c: (b * nc + c, colblk))

    state_specs = [pl.BlockSpec((1, H, d, d), lambda b, c: (b, 0, 0, 0)),
                   pl.BlockSpec((H, 1, d), lambda b, c: (b, 0, 0)),
                   pl.BlockSpec((H, 1, 1), lambda b, c: (b, 0, 0))]
    in_specs = [qkvo(COL_MQ), qkvo(COL_MK), qkvo(COL_MV), qkvo(COL_MO),
                pl.BlockSpec((L, LANES), lambda b, c: (b * nc + c, 0)),
                pl.BlockSpec((None, 1, LANES), lambda b, c: (layer, 0, 0)),
                pl.BlockSpec((None, 1, W), lambda b, c: (layer, 0, 0))]
    args = [proj, proj, proj, proj, gates, b_gates, g_heads]
    if has_init:
        c0, n0, m0 = init
        depth = c0.shape[0]
        in_specs += [pl.BlockSpec((None, 1, H, d, d), lambda b, c: (layer, b, 0, 0, 0)),
                     pl.BlockSpec((None, H, 1, d), lambda b, c: (layer, b, 0, 0)),
                     pl.BlockSpec((None, H, 1, 1), lambda b, c: (layer, b, 0, 0))]
        args += [c0, n0.reshape(depth, B * H, 1, d), m0.reshape(depth, B * H, 1, 1)]
    mix, c, n, m = pl.pallas_call(
        functools.partial(_mlstm_kernel, L=L, has_init=has_init),
        grid=(B, nc),
        in_specs=in_specs,
        out_specs=[pl.BlockSpec((L, W), lambda b, c: (b * nc + c, 0))] + state_specs,
        out_shape=[jax.ShapeDtypeStruct((B * T, W), mix_dtype),
                   jax.ShapeDtypeStruct((B, H, d, d), F32),
                   jax.ShapeDtypeStruct((B * H, 1, d), F32),
                   jax.ShapeDtypeStruct((B * H, 1, 1), F32)],
        compiler_params=_params(("parallel", "arbitrary")),
        name="mlstm",
    )(*args)
    return mix, c, n.reshape(B, H, d), m.reshape(B, H)


def _sb_block(z, u, carry, mask):
    sp = _softplus(z)
    lk = -sp
    if mask is not None:
        lk = jnp.where(mask, lk, 0.0)
    after = _split_dot(lk, u)
    a = jnp.exp((z - sp) + after + carry)
    if mask is not None:
        a = jnp.where(mask, a, 0.0)
    return a, carry + (after[:, 0:1] + lk[:, 0:1])


def _sb_prompt_kernel(bias_ref, ti_ref, tj_ref, q_ref, k_ref, v_ref, gn_ref, o_ref,
                      qb_ref, kb_ref, vb_ref, madd_ref, ls_ref, lk_ref, e_ref, acc_ref, carry_ref, *,
                      TQ, n_tiles, layer):
    h = pl.program_id(1)
    bias = bias_ref[layer, h]
    scale = SB_DIM ** -0.5
    row = lax.broadcasted_iota(jnp.int32, (TQ, TQ), 0)
    col = lax.broadcasted_iota(jnp.int32, (TQ, TQ), 1)
    u = (row > col).astype(BF16)

    qb_ref[...] = q_ref[...].astype(BF16)
    kb_ref[...] = k_ref[...].astype(BF16)
    vb_ref[...] = v_ref[...].astype(BF16)
    madd_ref[0] = jnp.full((TQ, TQ), bias, F32)
    madd_ref[1] = jnp.where(col < row, bias, M_EMPTY)
    ls_ref[...] = jnp.full((TQ, TQ), M_EMPTY, F32)
    lk_ref[...] = jnp.zeros((TQ, TQ), F32)
    e_ref[...] = jnp.full((TQ, TQ), M_EMPTY, F32)
    acc_ref[...] = jnp.zeros(acc_ref.shape, F32)
    carry_ref[...] = jnp.zeros(carry_ref.shape, F32)

    def rows(blk):
        return pl.ds(pl.multiple_of(blk * TQ, TQ), TQ)

    def body(t, tot):
        ia, ja = ti_ref[t], tj_ref[t]
        z = lax.dot_general(qb_ref[rows(ia), :], kb_ref[rows(ja), :], NT_DIMS, preferred_element_type=F32)
        z = z * scale + madd_ref[(ia == ja).astype(jnp.int32)]
        tc = jnp.maximum(t - 2, 0)
        ic, jc = ti_ref[tc], tj_ref[tc]
        a = jnp.exp(e_ref[...] + carry_ref[...])
        acc_ref[...] += jnp.dot(a.astype(BF16), vb_ref[rows(jc), :], preferred_element_type=F32)
        carry_ref[...] += tot
        lk = lk_ref[...]
        after = _split_dot(lk, u)
        e_ref[...] = ls_ref[...] + after
        tot = after[:, 0:1] + lk[:, 0:1]
        sp = _softplus(z)
        ls_ref[...] = z - sp
        lk_ref[...] = -sp

        @pl.when(jnp.logical_and(t >= 2, jc == 0))
        def _():
            o_ref[rows(ic), :] = _rmsnorm(acc_ref[...], gn_ref[...]).astype(o_ref.dtype)
            acc_ref[...] = jnp.zeros(acc_ref.shape, F32)
            carry_ref[...] = jnp.zeros(carry_ref.shape, F32)

        return tot

    lax.fori_loop(0, n_tiles + 2, body, jnp.zeros((TQ, 1), F32))


def _sb_prompt(proj, b_sb, g_heads, layer, B, T):
    TQ = min(256, T)
    nq = T // TQ
    H, d = SB_HEADS, SB_DIM
    tiles = [(i, j) for i in range(nq) for j in range(i, -1, -1)]
    tiles += [tiles[-1]] * 2
    ti = jnp.asarray([t[0] for t in tiles], jnp.int32)
    tj = jnp.asarray([t[1] for t in tiles], jnp.int32)
    smem = pl.BlockSpec(memory_space=pltpu.SMEM)
    return pl.pallas_call(
        functools.partial(_sb_prompt_kernel, TQ=TQ, n_tiles=len(tiles) - 2, layer=layer),
        grid=(B, H),
        in_specs=[smem, smem, smem,
                  pl.BlockSpec((T, d), lambda b, h: (b, COL_SQ * H + h)),
                  pl.BlockSpec((T, d), lambda b, h: (b, COL_SK * H + h)),
                  pl.BlockSpec((T, d), lambda b, h: (b, COL_SV * H + h)),
                  pl.BlockSpec((None, 1, d), lambda b, h: (layer, 0, ML_WIDTH // d + h))],
        out_specs=pl.BlockSpec((T, d), lambda b, h: (b, h)),
        out_shape=jax.ShapeDtypeStruct((B * T, SB_WIDTH), BF16),
        scratch_shapes=[pltpu.VMEM((T, d), BF16), pltpu.VMEM((T, d), BF16), pltpu.VMEM((T, d), BF16),
                        pltpu.VMEM((2, TQ, TQ), F32),
                        pltpu.VMEM((TQ, TQ), F32), pltpu.VMEM((TQ, TQ), F32), pltpu.VMEM((TQ, TQ), F32),
                        pltpu.VMEM((TQ, d), F32), pltpu.VMEM((TQ, 1), F32)],
        compiler_params=_params(("parallel", "parallel")),
        name="sb_prompt",
    )(b_sb, ti, tj, proj, proj, proj, g_heads)


def _sb_sample_kernel(pt_ref, q_ref, kn_ref, vn_ref, bias_ref, gn_ref, *rest, T, G):
    k_refs, v_refs = rest[:G], rest[G:2 * G]
    o_ref, acc_ref, carry_ref = rest[2 * G:]
    H, d = SB_HEADS, SB_DIM
    R = H * T
    s_id = pl.program_id(1)
    scale = d ** -0.5
    row = lax.broadcasted_iota(jnp.int32, (PAGE, PAGE), 0)
    col = lax.broadcasted_iota(jnp.int32, (PAGE, PAGE), 1)
    u = (row > col).astype(BF16)
    qb = q_ref[...].astype(BF16)
    bias = bias_ref[...]

    def logits(keys_of_head):
        z = [lax.dot_general(qb[h * T:(h + 1) * T], keys_of_head(h), NT_DIMS, preferred_element_type=F32)
             for h in range(H)]
        return jnp.concatenate(z, axis=0) * scale + bias

    def weighted_values(a, values_of_head):
        ab = a.astype(BF16)
        o = [jnp.dot(ab[h * T:(h + 1) * T], values_of_head(h), preferred_element_type=F32) for h in range(H)]
        return jnp.concatenate(o, axis=0)

    @pl.when(s_id == 0)
    def _():
        zpad = jnp.zeros((PAGE - T, d), F32)
        new_rows = lambda ref: lambda h: jnp.concatenate([ref[h * T:(h + 1) * T, :], zpad], axis=0).astype(BF16)
        z = logits(new_rows(kn_ref))
        mask = (lax.broadcasted_iota(jnp.int32, (R, PAGE), 1) < lax.broadcasted_iota(jnp.int32, (R, PAGE), 0) % T)
        a, carry = _sb_block(z, u, jnp.zeros((R, 1), F32), mask)
        carry_ref[...] = carry
        acc_ref[...] = weighted_values(a, new_rows(vn_ref))

    def page_rows(refs):
        return lambda h: jnp.concatenate([r[pl.ds(h, PAGE, stride=H), :] for r in refs], axis=0).astype(BF16)

    z = logits(page_rows(k_refs))
    sp = _softplus(z)
    lk = -sp
    lk_st = jnp.concatenate([lk[:, g * PAGE:(g + 1) * PAGE] for g in range(G)], axis=0)
    after_st = _split_dot(lk_st, u)
    tot_st = after_st[:, 0:1] + lk_st[:, 0:1]
    carry = carry_ref[...]
    after = []
    for g in range(G):
        after.append(after_st[g * R:(g + 1) * R] + carry)
        carry = carry + tot_st[g * R:(g + 1) * R]
    carry_ref[...] = carry
    a = jnp.exp((z - sp) + jnp.concatenate(after, axis=1))
    acc_ref[...] += weighted_values(a, page_rows(v_refs))

    @pl.when(s_id == pl.num_programs(1) - 1)
    def _():
        o_ref[...] = _rmsnorm(acc_ref[...], gn_ref[...])


def _sb_sample(q, k_new, v_new, cache_k, cache_v, layer, page_table, bias_col, gn_rows, B, T):
    n_pages = page_table.shape[1]
    G = 16 if n_pages % 16 == 0 else (8 if n_pages % 8 == 0 else 1)
    nsteps = n_pages // G
    H, d = SB_HEADS, SB_DIM
    R = H * T

    def page_spec(g):
        return pl.BlockSpec((None, None, PAGE * H, d),
                            lambda b, s, pt: (layer, pt[b, n_pages - 1 - (s * G + g)], 0, 0))

    row_spec = pl.BlockSpec((None, R, d), lambda b, s, pt: (b, 0, 0))
    grid_spec = pltpu.PrefetchScalarGridSpec(
        num_scalar_prefetch=1,
        grid=(B, nsteps),
        in_specs=[row_spec, row_spec, row_spec,
                  pl.BlockSpec((None, R, 1), lambda b, s, pt: (layer, 0, 0)),
                  pl.BlockSpec((None, R, d), lambda b, s, pt: (layer, 0, 0))]
                 + [page_spec(g) for g in range(G)] + [page_spec(g) for g in range(G)],
        out_specs=row_spec,
        scratch_shapes=[pltpu.VMEM((R, d), F32), pltpu.VMEM((R, 1), F32)],
    )
    return pl.pallas_call(
        functools.partial(_sb_sample_kernel, T=T, G=G),
        grid_spec=grid_spec,
        out_shape=jax.ShapeDtypeStruct((B, R, d), F32),
        compiler_params=_params(("parallel", "arbitrary")),
        name="sb_sample",
    )(page_table, q, k_new, v_new, bias_col, gn_rows, *([cache_k] * G), *([cache_v] * G))


def _out_proj_kernel(a1_ref, a2_ref, w1_ref, w2_ref, x_ref, o_ref):
    o_ref[...] = (x_ref[...]
                  + jnp.dot(a1_ref[...].astype(BF16), w1_ref[...], preferred_element_type=F32)
                  + jnp.dot(a2_ref[...].astype(BF16), w2_ref[...], preferred_element_type=F32))


def _out_proj(a1, a2, w, layer, x, tm):
    M, D = x.shape
    K1, K2 = a1.shape[1], a2.shape[1]
    return pl.pallas_call(
        _out_proj_kernel,
        grid=(M // tm,),
        in_specs=[pl.BlockSpec((tm, K1), lambda i: (i, 0)),
                  pl.BlockSpec((tm, K2), lambda i: (i, 0)),
                  pl.BlockSpec((None, K1, D), lambda i: (layer, 0, 0)),
                  pl.BlockSpec((None, K2, D), lambda i: (layer, K1 // K2, 0)),
                  pl.BlockSpec((tm, D), lambda i: (i, 0))],
        out_specs=pl.BlockSpec((tm, D), lambda i: (i, 0)),
        out_shape=jax.ShapeDtypeStruct((M, D), F32),
        compiler_params=_params(("parallel",)),
        name="out_proj",
    )(a1, a2, w, w, x)


def _cross_kernel(x_ref, g_ref, wq_ref, mk_ref, mv_ref, wo_ref, o_ref):
    x = x_ref[...]
    hn = _rmsnorm(x, g_ref[...]).astype(BF16)
    q = jnp.dot(hn, wq_ref[...], preferred_element_type=F32)
    scale = X_DIM ** -0.5
    outs = []
    for h in range(X_HEADS):
        sl = slice(h * X_DIM, (h + 1) * X_DIM)
        s = lax.dot_general(q[:, sl].astype(BF16), mk_ref[:, sl].astype(BF16), NT_DIMS,
                            preferred_element_type=F32) * scale
        e = jnp.exp(s - jnp.max(s, axis=-1, keepdims=True))
        p = e / jnp.sum(e, axis=-1, keepdims=True)
        outs.append(jnp.dot(p.astype(BF16), mv_ref[:, sl].astype(BF16), preferred_element_type=F32))
    o = jnp.concatenate(outs, axis=1).astype(BF16)
    o_ref[...] = x + jnp.dot(o, wo_ref[...], preferred_element_type=F32)


def _cross(x, g, wq, wo, layer, mk, mv, mk_spec, mv_spec, tm):
    M, D = x.shape
    return pl.pallas_call(
        _cross_kernel,
        grid=(M // tm,),
        in_specs=[pl.BlockSpec((tm, D), lambda i: (i, 0)),
                  pl.BlockSpec((None, 1, D), lambda i: (layer, 0, 0)),
                  pl.BlockSpec((None, D, X_WIDTH), lambda i: (layer, 0, 0)),
                  mk_spec, mv_spec,
                  pl.BlockSpec((None, X_WIDTH, D), lambda i: (layer, 0, 0))],
        out_specs=pl.BlockSpec((tm, D), lambda i: (i, 0)),
        out_shape=jax.ShapeDtypeStruct((M, D), F32),
        compiler_params=_params(("parallel",)),
        name="cross",
    )(x, g, wq, mk, mv, wo)


def _ffn_kernel(*refs, tm, from_buf, tiles_per_seq, final):
    refs = list(refs)
    x_ref = refs.pop(0)
    xh_ref = None if from_buf else refs.pop(0)
    g_ref, wa_ref, wb_ref, cw_ref, cb_ref, wd_ref = refs[:6]
    refs = refs[6:]
    buf_ref = refs.pop(0) if from_buf else None
    gf_ref = refs.pop(0) if final else None
    o_ref, st_ref, hn_ref, acc_ref = refs
    HALO = SUBLANES
    f = pl.program_id(1)

    @pl.when(f == 0)
    def _():
        hn_ref[HALO:, :] = _rmsnorm(x_ref[...], g_ref[...]).astype(BF16)
        if not from_buf:
            hn_ref[:HALO, :] = _rmsnorm(xh_ref[...], g_ref[...]).astype(BF16)
        acc_ref[...] = jnp.zeros(acc_ref.shape, F32)

    tf = wa_ref.shape[1]
    if from_buf:
        a = jnp.dot(hn_ref[HALO:, :], wa_ref[...], preferred_element_type=F32)
        nb = buf_ref.shape[1]
        a_ext = jnp.concatenate([buf_ref[j] for j in range(CONV_W - 1)] + [a], axis=0)
        taps = [a_ext[j * nb:j * nb + tm, :] for j in range(CONV_W)]
    else:
        a_ext = jnp.dot(hn_ref[...], wa_ref[...], preferred_element_type=F32)
        seq_start = (pl.program_id(0) % tiles_per_seq) == 0
        rows = lax.broadcasted_iota(jnp.int32, (HALO + tm, 1), 0)
        a_ext = jnp.where(jnp.logical_and(seq_start, rows < HALO), 0.0, a_ext)
        taps = [pltpu.roll(a_ext, 2, axis=0)[HALO:, :], pltpu.roll(a_ext, 1, axis=0)[HALO:, :], a_ext[HALO:, :]]
    b = jnp.dot(hn_ref[HALO:, :], wb_ref[...], preferred_element_type=F32)
    c = cb_ref[...] + ((taps[0] * cw_ref[0:1, :] + taps[1] * cw_ref[1:2, :]) + taps[2] * cw_ref[2:3, :])
    gate = (c * jax.nn.sigmoid(c)) * b
    acc_ref[...] += jnp.dot(gate.astype(BF16), wd_ref[...], preferred_element_type=F32)
    st_ref[...] = taps[2][tm - st_ref.shape[0]:, :]

    @pl.when(f == pl.num_programs(1) - 1)
    def _():
        y = x_ref[...] + acc_ref[...]
        o_ref[...] = _rmsnorm(y, gf_ref[...]) if final else y


def _ffn(x, g, w_up, conv_w, conv_b, w_down, layer, B, T, tm, tf, buf=None, final_gain=None):
    M, D = x.shape
    F = w_down.shape[1]
    nf = F // tf
    from_buf = buf is not None
    assert F % tf == 0 and tm % SUBLANES == 0 and (tm == M and B % SUBLANES == 0 if from_buf else T % tm == 0)
    tiles_per_seq = 1 if from_buf else T // tm
    st_rows = (CONV_W - 1) * B if from_buf else SUBLANES
    hb = tm // SUBLANES
    in_specs = [pl.BlockSpec((tm, D), lambda i, f: (i, 0))]
    args = [x]
    if not from_buf:
        in_specs.append(pl.BlockSpec((SUBLANES, D), lambda i, f: (jnp.maximum(i * hb - 1, 0), 0)))
        args.append(x)
    in_specs += [pl.BlockSpec((None, 1, D), lambda i, f: (layer, 0, 0)),
                 pl.BlockSpec((None, D, tf), lambda i, f: (layer, 0, f)),
                 pl.BlockSpec((None, D, tf), lambda i, f: (layer, 0, nf + f)),
                 pl.BlockSpec((None, CONV_W, tf), lambda i, f: (layer, 0, f)),
                 pl.BlockSpec((None, 1, tf), lambda i, f: (layer, 0, f)),
                 pl.BlockSpec((None, tf, D), lambda i, f: (layer, f, 0))]
    args += [g, w_up, w_up, conv_w, conv_b, w_down]
    if from_buf:
        in_specs.append(pl.BlockSpec((None, CONV_W - 1, B, tf), lambda i, f: (layer, 0, 0, f)))
        args.append(buf)
    if final_gain is not None:
        in_specs.append(pl.BlockSpec((1, D), lambda i, f: (0, 0)))
        args.append(final_gain.reshape(1, D))
    y, st = pl.pallas_call(
        functools.partial(_ffn_kernel, tm=tm, from_buf=from_buf, tiles_per_seq=tiles_per_seq,
                          final=final_gain is not None),
        grid=(M // tm, nf),
        in_specs=in_specs,
        out_specs=[pl.BlockSpec((tm, D), lambda i, f: (i, 0)),
                   pl.BlockSpec((None, st_rows, tf), lambda i, f: (i, 0, f))],
        out_shape=[jax.ShapeDtypeStruct((M, D), F32),
                   jax.ShapeDtypeStruct((M // tm, st_rows, F), F32)],
        scratch_shapes=[pltpu.VMEM((SUBLANES + tm, D), BF16), pltpu.VMEM((tm, D), F32)],
        compiler_params=_params(("parallel", "arbitrary")),
        name="ffn",
    )(*args)
    if from_buf:
        return y, st.reshape(CONV_W - 1, B, F).transpose(1, 0, 2)
    return y, st[tiles_per_seq - 1::tiles_per_seq, SUBLANES - (CONV_W - 1):, :]


def _prep_params(T_s, w_in, b_gates, b_sb, norm_mix, norm_heads, w_out, norm_cross, norm_mem, wx_q, wx_kv, wx_o,
                 norm_ffn, w_up, conv_w, conv_b, w_down):
    depth, D, _ = w_in.shape
    H, d = SB_HEADS, SB_DIM
    g0 = 4 * ML_WIDTH
    g1 = g0 + 2 * ML_HEADS
    pad = LANES - 2 * ML_HEADS
    row = lambda a: a.reshape(depth, 1, -1)
    return dict(
        w_ml=w_in[:, :, :g0].astype(BF16), w_sb=w_in[:, :, g1:].astype(BF16),
        w_gate=jnp.concatenate([w_in[:, :, g0:g1], jnp.zeros((depth, D, pad), F32)], axis=2).astype(BF16),
        bg=row(jnp.concatenate([b_gates, jnp.zeros((depth, pad), F32)], axis=1)),
        w_out=w_out.astype(BF16), wx_q=wx_q.astype(BF16), wx_kv=wx_kv.astype(BF16), wx_o=wx_o.astype(BF16),
        w_up=w_up.astype(BF16), w_down=w_down.astype(BF16),
        norm_mix=row(norm_mix), norm_heads=row(norm_heads), norm_cross=row(norm_cross), norm_mem=row(norm_mem),
        norm_ffn=row(norm_ffn), conv_w=conv_w, conv_b=row(conv_b), b_sb=b_sb,
        bias_col=jnp.repeat(b_sb, T_s, axis=1).reshape(depth, H * T_s, 1),
        gn_rows=jnp.repeat(norm_heads[:, ML_WIDTH:].reshape(depth, H, d), T_s, axis=1))


def _mix_inputs(x, layer, p, tm):
    return _in_proj(x, p["norm_mix"], p["w_ml"], p["w_sb"], p["w_gate"], layer, tm, 1024)


def _prompt_layer(x, mem, B, T, layer, p, final_gain=None):
    n_mem = mem.shape[0] // B
    kv = _norm_matmul(mem, p["norm_mem"], p["wx_kv"], layer, min(512, mem.shape[0]), 512)
    tm = min(512, T)
    proj, gates = _mix_inputs(x, layer, p, min(1024, T))
    mix_ml, c, n, m = _mlstm(proj, gates, p["bg"], p["norm_heads"], layer, B, T, min(LANES, T), None, BF16)
    mix_sb = _sb_prompt(proj, p["b_sb"], p["norm_heads"], layer, B, T)
    x = _out_proj(mix_ml, mix_sb, p["w_out"], layer, x, tm)
    tpb = T // tm
    x = _cross(x, p["norm_cross"], p["wx_q"], p["wx_o"], layer, kv, kv,
               pl.BlockSpec((n_mem, X_WIDTH), lambda i: (i // tpb, 0)),
               pl.BlockSpec((n_mem, X_WIDTH), lambda i: (i // tpb, 1)), tm)
    x, st = _ffn(x, p["norm_ffn"], p["w_up"], p["conv_w"], p["conv_b"], p["w_down"], layer, B, T, tm, 512,
                 final_gain=final_gain)
    return x, (proj, c, n, m, st, kv)


def _sample_layer(x, B, T, layer, p, cache_k, cache_v, page_table, state, conv_state, mem_k, mem_v,
                  final_gain=None):
    M = B * T
    H, d = SB_HEADS, SB_DIM
    proj, gates = _mix_inputs(x, layer, p, M)
    mix_ml, c, n, m = _mlstm(proj, gates, p["bg"], p["norm_heads"], layer, B, T, T, state, F32)
    heads = lambda colblk: proj[:, colblk * SB_WIDTH:(colblk + 1) * SB_WIDTH].reshape(B, T, H, d)
    sbk, sbv = heads(COL_SK), heads(COL_SV)
    head_rows = lambda a: a.transpose(0, 2, 1, 3).reshape(B, H * T, d)
    mix_sb = _sb_sample(head_rows(heads(COL_SQ)), head_rows(sbk), head_rows(sbv), cache_k, cache_v, layer,
                        page_table, p["bias_col"], p["gn_rows"], B, T)
    mix_sb = mix_sb.reshape(B, H, T, d).transpose(0, 2, 1, 3).reshape(M, SB_WIDTH)
    x = _out_proj(mix_ml, mix_sb, p["w_out"], layer, x, M)
    n_mem = mem_k.shape[2]
    mem_spec = pl.BlockSpec((None, None, n_mem, X_WIDTH), lambda i: (layer, i, 0, 0))
    x = _cross(x, p["norm_cross"], p["wx_q"], p["wx_o"], layer, mem_k, mem_v, mem_spec, mem_spec, T)
    time_major = lambda a, n0, n1: a.reshape(n0, n1, -1).transpose(1, 0, 2).reshape(M, -1)
    x, st = _ffn(time_major(x, B, T), p["norm_ffn"], p["w_up"], p["conv_w"], p["conv_b"], p["w_down"], layer,
                 B, T, M, 512, buf=conv_state, final_gain=final_gain)
    x = time_major(x, T, B)
    return x, (sbk, sbv, c, n, m, st)


def kernel(x_prompt, x_sample, mem_prompt, cache_sb_k, cache_sb_v, state_mlstm_c, state_mlstm_n, state_mlstm_m,
           state_ffn_conv, cache_mem_k, cache_mem_v, page_table, norm_mix, w_in, b_gates, b_sb, norm_heads, w_out,
           norm_cross, norm_mem, wx_q, wx_kv, wx_o, norm_ffn, w_up, conv_w, conv_b, w_down, norm_final):
    Bp, Tp, D = x_prompt.shape
    Bs, Ts, _ = x_sample.shape
    depth = w_in.shape[0]
    n_mem = cache_mem_k.shape[2]
    n_pool = cache_sb_k.shape[1]
    p = _prep_params(Ts, w_in, b_gates, b_sb, norm_mix, norm_heads, w_out, norm_cross, norm_mem, wx_q, wx_kv, wx_o,
                     norm_ffn, w_up, conv_w, conv_b, w_down)
    state = (state_mlstm_c, state_mlstm_n, state_mlstm_m)
    conv_state = state_ffn_conv.transpose(0, 2, 1, 3)
    cache_k = cache_sb_k.reshape(depth, n_pool, PAGE * SB_HEADS, SB_DIM)
    cache_v = cache_sb_v.reshape(depth, n_pool, PAGE * SB_HEADS, SB_DIM)
    mem_k = cache_mem_k.reshape(depth, Bs, n_mem, X_WIDTH)
    mem_v = cache_mem_v.reshape(depth, Bs, n_mem, X_WIDTH)
    yp = x_prompt.reshape(Bp * Tp, D)
    ys = x_sample.reshape(Bs * Ts, D)
    mem = mem_prompt.reshape(Bp * mem_prompt.shape[1], D)
    P = [[] for _ in range(6)]
    S = [[] for _ in range(6)]
    for l in range(depth):
        gf = norm_final if l == depth - 1 else None
        yp, st_p = _prompt_layer(yp, mem, Bp, Tp, l, p, gf)
        for lst, a in zip(P, st_p):
            lst.append(a)
        ys, st_s = _sample_layer(ys, Bs, Ts, l, p, cache_k, cache_v, page_table, state, conv_state, mem_k, mem_v,
                                 gf)
        for lst, a in zip(S, st_s):
            lst.append(a)
    y_prompt = yp.reshape(Bp, Tp, D)
    y_sample = ys.reshape(Bs, Ts, D)
    projs, pc, pn, pm, pconv, kvs = P
    cols = lambda a, j: jnp.stack([x[:, j * SB_WIDTH:(j + 1) * SB_WIDTH] for x in a])
    p_sb_k = cols(projs, COL_SK).reshape(depth, Bp, Tp, SB_HEADS, SB_DIM)
    p_sb_v = cols(projs, COL_SV).reshape(depth, Bp, Tp, SB_HEADS, SB_DIM)
    kv = jnp.stack(kvs)
    p_mem_k = kv[:, :, :X_WIDTH].reshape(depth, Bp, n_mem, X_HEADS, X_DIM)
    p_mem_v = kv[:, :, X_WIDTH:].reshape(depth, Bp, n_mem, X_HEADS, X_DIM)
    return ((y_prompt, y_sample, p_sb_k, p_sb_v, jnp.stack(pc), jnp.stack(pn), jnp.stack(pm), jnp.stack(pconv),
             p_mem_k, p_mem_v) + tuple(jnp.stack(a) for a in S))
```

```python
import functools

import jax
import jax.numpy as jnp
from jax import lax
from jax.experimental import pallas as pl
from jax.experimental.pallas import tpu as pltpu

F32 = jnp.float32
BF16 = jnp.bfloat16

EPS = 1e-6
M_EMPTY = -1e30
ML_HEADS = 4
ML_DIM = 256
SB_HEADS = 8
SB_DIM = 128
X_HEADS = 4
X_DIM = 128
PAGE = 128
CONV_W = 3
ML_WIDTH = ML_HEADS * ML_DIM
SB_WIDTH = SB_HEADS * SB_DIM
X_WIDTH = X_HEADS * X_DIM

SUBLANES = 8
LANES = 128
MIB = 1024 * 1024
VMEM_LIMIT = 56 * MIB

COL_MQ, COL_MK, COL_MV, COL_MO = 0, 1, 2, 3
COL_SQ, COL_SK, COL_SV = 4, 5, 6
N_PROJ = 7 * ML_WIDTH

NT_DIMS = (((1,), (1,)), ((), ()))
TN_DIMS = (((0,), (0,)), ((), ()))


def _params(sem):
    return pltpu.CompilerParams(dimension_semantics=sem, vmem_limit_bytes=VMEM_LIMIT)


def _rmsnorm(xf, g):
    return xf * lax.rsqrt(jnp.mean(xf * xf, axis=-1, keepdims=True) + EPS) * g


def _log_sigmoid(x):
    return jnp.minimum(x, 0.0) - jnp.log1p(jnp.exp(-jnp.abs(x)))


def _softplus(z):
    t = jnp.exp(-jnp.abs(z))
    u = 1.0 + t
    return jnp.maximum(z, 0.0) + (jnp.log(u) + (t - (u - 1.0)))


def _split_dot(x, u):
    hi = x.astype(BF16)
    lo = (x - hi.astype(F32)).astype(BF16)
    return (jnp.dot(hi, u, preferred_element_type=F32) + jnp.dot(lo, u, preferred_element_type=F32))


def _norm_matmul_kernel(x_ref, g_ref, w_ref, o_ref, hn_ref):
    @pl.when(pl.program_id(1) == 0)
    def _():
        hn_ref[...] = _rmsnorm(x_ref[...], g_ref[...]).astype(BF16)

    o_ref[...] = jnp.dot(hn_ref[...], w_ref[...], preferred_element_type=F32)


def _norm_matmul(x, g, w, layer, tm, tn):
    M, D = x.shape
    N = w.shape[2]
    assert M % tm == 0 and N % tn == 0
    return pl.pallas_call(
        _norm_matmul_kernel,
        grid=(M // tm, N // tn),
        in_specs=[pl.BlockSpec((tm, D), lambda i, j: (i, 0)),
                  pl.BlockSpec((None, 1, D), lambda i, j: (layer, 0, 0)),
                  pl.BlockSpec((None, D, tn), lambda i, j: (layer, 0, j))],
        out_specs=pl.BlockSpec((tm, tn), lambda i, j: (i, j)),
        out_shape=jax.ShapeDtypeStruct((M, N), F32),
        scratch_shapes=[pltpu.VMEM((tm, D), BF16)],
        compiler_params=_params(("parallel", "arbitrary")),
        name="norm_matmul",
    )(x, g, w)


def _in_proj_kernel(x_ref, g_ref, wml_ref, wsb_ref, wg_ref, o_ref, og_ref, hn_ref, *, n_ml):
    n = pl.program_id(1)

    @pl.when(n == 0)
    def _():
        hn_ref[...] = _rmsnorm(x_ref[...], g_ref[...]).astype(BF16)
        og_ref[...] = jnp.dot(hn_ref[...], wg_ref[...], preferred_element_type=F32)

    @pl.when(n < n_ml)
    def _():
        o_ref[...] = jnp.dot(hn_ref[...], wml_ref[...], preferred_element_type=F32)

    @pl.when(n >= n_ml)
    def _():
        o_ref[...] = jnp.dot(hn_ref[...], wsb_ref[...], preferred_element_type=F32)


def _in_proj(x, g, w_ml, w_sb, w_gate, layer, tm, tn):
    M, D = x.shape
    n_ml, n_sb = w_ml.shape[2] // tn, w_sb.shape[2] // tn
    assert M % tm == 0 and w_ml.shape[2] % tn == 0 and w_sb.shape[2] % tn == 0
    return pl.pallas_call(
        functools.partial(_in_proj_kernel, n_ml=n_ml),
        grid=(M // tm, n_ml + n_sb),
        in_specs=[pl.BlockSpec((tm, D), lambda i, j: (i, 0)),
                  pl.BlockSpec((None, 1, D), lambda i, j: (layer, 0, 0)),
                  pl.BlockSpec((None, D, tn), lambda i, j: (layer, 0, jnp.minimum(j, n_ml - 1))),
                  pl.BlockSpec((None, D, tn), lambda i, j: (layer, 0, jnp.maximum(j - n_ml, 0))),
                  pl.BlockSpec((None, D, LANES), lambda i, j: (layer, 0, 0))],
        out_specs=[pl.BlockSpec((tm, tn), lambda i, j: (i, j)),
                   pl.BlockSpec((tm, LANES), lambda i, j: (i, 0))],
        out_shape=[jax.ShapeDtypeStruct((M, (n_ml + n_sb) * tn), F32),
                   jax.ShapeDtypeStruct((M, LANES), F32)],
        scratch_shapes=[pltpu.VMEM((tm, D), BF16)],
        compiler_params=_params(("parallel", "arbitrary")),
        name="in_proj",
    )(x, g, w_ml, w_sb, w_gate)


def _mlstm_kernel(q_ref, k_ref, v_ref, o_ref, gt_ref, bg_ref, gn_ref, *rest, L, has_init):
    if has_init:
        c0_ref, n0_ref, m0_ref, mix_ref, c_ref, n_ref, m_ref = rest
    else:
        mix_ref, c_ref, n_ref, m_ref = rest
    S = LANES
    H, d = ML_HEADS, ML_DIM

    @pl.when(pl.program_id(1) == 0)
    def _():
        if has_init:
            c_ref[...] = c0_ref[...]
            n_ref[...] = n0_ref[...]
            m_ref[...] = m0_ref[...]
        else:
            c_ref[...] = jnp.zeros(c_ref.shape, F32)
            n_ref[...] = jnp.zeros(n_ref.shape, F32)
            m_ref[...] = jnp.full(m_ref.shape, M_EMPTY, F32)

    def pad(a):
        if L == S:
            return a
        return jnp.concatenate([a, jnp.zeros((S - L, a.shape[1]), a.dtype)], axis=0)

    gates = pad(gt_ref[...] + bg_ref[...])
    row = lax.broadcasted_iota(jnp.int32, (S, S), 0)
    col = lax.broadcasted_iota(jnp.int32, (S, S), 1)
    causal = col <= row
    valid = lax.broadcasted_iota(jnp.int32, (S, 1), 0) < L
    b_all = jnp.dot(causal.astype(F32), _log_sigmoid(gates), precision=lax.Precision.HIGHEST,
                    preferred_element_type=F32)
    r_t = (gates - pltpu.roll(b_all, LANES - H, axis=1)).T

    for h in range(H):
        hs = slice(h * d, (h + 1) * d)
        i_col = gates[:, h:h + 1]
        b_col = b_all[:, H + h:H + h + 1]
        g_col = b_col + m_ref[h]
        logd = b_col + r_t[h:h + 1, :]
        mt = jnp.maximum(g_col, jnp.max(jnp.where(causal, logd, -jnp.inf), axis=-1, keepdims=True))

        q = pad(q_ref[:, hs])
        k = pad(k_ref[:, hs]) * (d ** -0.5)
        v = pad(v_ref[:, hs])
        qb, kb = q.astype(BF16), k.astype(BF16)
        s = lax.dot_general(q, k, NT_DIMS, precision=lax.Precision.HIGHEST, preferred_element_type=F32)
        s = jnp.where(causal, s * jnp.exp(logd - mt), 0.0)
        gw = jnp.exp(g_col - mt)
        C = c_ref[0, h]
        n_row = n_ref[h]
        num = (jnp.dot(s.astype(BF16), v.astype(BF16), preferred_element_type=F32)
               + gw * jnp.dot(qb, C.astype(BF16), preferred_element_type=F32))
        den = jnp.sum(s, axis=-1, keepdims=True) + gw * jnp.sum(q * n_row, axis=-1, keepdims=True)
        den = jnp.maximum(jnp.abs(den), jnp.exp(-mt))
        hcell = num / den

        m_new = mt[L - 1:L, :]
        wk = jnp.where(valid, jnp.exp(b_col[L - 1:L, :] - b_col + i_col - m_new), 0.0)
        decay = jnp.exp(g_col[L - 1:L, :] - m_new)
        c_ref[0, h] = decay * C + lax.dot_general(kb, (wk * v).astype(BF16), TN_DIMS,
                                                  preferred_element_type=F32)
        n_ref[h] = decay * n_row + jnp.sum(wk * k, axis=0, keepdims=True)
        m_ref[h] = m_new

        hm = jax.nn.sigmoid(o_ref[:, hs]) * hcell[:L]
        mix_ref[:, hs] = _rmsnorm(hm, gn_ref[:, hs]).astype(mix_ref.dtype)


def _mlstm(proj, gates, b_gates, g_heads, layer, B, T, L, init, mix_dtype):
    nc = T // L
    H, d = ML_HEADS, ML_DIM
    W = ML_WIDTH
    has_init = init is not None

    def qkvo(colblk):
        return pl.BlockSpec((L, W), lambda b, c: (b * nc + c, colblk))

    state_specs = [pl.BlockSpec((1, H, d, d), lambda b, c: (b, 0, 0, 0)),
                   pl.BlockSpec((H, 1, d), lambda b, c: (b, 0, 0)),
                   pl.BlockSpec((H, 1, 1), lambda b, c: (b, 0, 0))]
    in_specs = [qkvo(COL_MQ), qkvo(COL_MK), qkvo(COL_MV), qkvo(COL_MO),
                pl.BlockSpec((L, LANES), lambda b, c: (b * nc + c, 0)),
                pl.BlockSpec((None, 1, LANES), lambda b, c: (layer, 0, 0)),
                pl.BlockSpec((None, 1, W), lambda b, c: (layer, 0, 0))]
    args = [proj, proj, proj, proj, gates, b_gates, g_heads]
    if has_init:
        c0, n0, m0 = init
        depth = c0.shape[0]
        in_specs += [pl.BlockSpec((None, 1, H, d, d), lambda b, c: (layer, b, 0, 0, 0)),
                     pl.BlockSpec((None, H, 1, d), lambda b, c: (layer, b, 0, 0)),
                     pl.BlockSpec((None, H, 1, 1), lambda b, c: (layer, b, 0, 0))]
        args += [c0, n0.reshape(depth, B * H, 1, d), m0.reshape(depth, B * H, 1, 1)]
    mix, c, n, m = pl.pallas_call(
        functools.partial(_mlstm_kernel, L=L, has_init=has_init),
        grid=(B, nc),
        in_specs=in_specs,
        out_specs=[pl.BlockSpec((L, W), lambda b, c: (b * nc + c, 0))] + state_specs,
        out_shape=[jax.ShapeDtypeStruct((B * T, W), mix_dtype),
                   jax.ShapeDtypeStruct((B, H, d, d), F32),
                   jax.ShapeDtypeStruct((B * H, 1, d), F32),
                   jax.ShapeDtypeStruct((B * H, 1, 1), F32)],
        compiler_params=_params(("parallel", "arbitrary")),
        name="mlstm",
    )(*args)
    return mix, c, n.reshape(B, H, d), m.reshape(B, H)


def _sb_block(z, u, carry, mask):
    sp = _softplus(z)
    lk = -sp
    if mask is not None:
        lk = jnp.where(mask, lk, 0.0)
    after = _split_dot(lk, u)
    a = jnp.exp((z - sp) + after + carry)
    if mask is not None:
        a = jnp.where(mask, a, 0.0)
    return a, carry + (after[:, 0:1] + lk[:, 0:1])


def _sb_prompt_kernel(bias_ref, ti_ref, tj_ref, q_ref, k_ref, v_ref, gn_ref, o_ref,
                      qb_ref, kb_ref, vb_ref, madd_ref, ls_ref, lk_ref, e_ref, acc_ref, carry_ref, *,
                      TQ, NH, n_tiles, layer):
    d = SB_DIM
    scale = d ** -0.5
    row = lax.broadcasted_iota(jnp.int32, (TQ, TQ), 0)
    col = lax.broadcasted_iota(jnp.int32, (TQ, TQ), 1)
    u = (row > col).astype(BF16)
    heads = range(NH)
    lanes = [slice(hh * d, (hh + 1) * d) for hh in heads]

    qb_ref[...] = q_ref[...].astype(BF16)
    kb_ref[...] = k_ref[...].astype(BF16)
    vb_ref[...] = v_ref[...].astype(BF16)
    for hh in heads:
        bias = bias_ref[layer, pl.program_id(1) * NH + hh]
        madd_ref[hh, 0] = jnp.full((TQ, TQ), bias, F32)
        madd_ref[hh, 1] = jnp.where(col < row, bias, M_EMPTY)
    ls_ref[...] = jnp.full(ls_ref.shape, M_EMPTY, F32)
    lk_ref[...] = jnp.zeros(lk_ref.shape, F32)
    e_ref[...] = jnp.full(e_ref.shape, M_EMPTY, F32)
    acc_ref[...] = jnp.zeros(acc_ref.shape, F32)
    carry_ref[...] = jnp.zeros(carry_ref.shape, F32)

    def rows(blk):
        return pl.ds(pl.multiple_of(blk * TQ, TQ), TQ)

    def body(t, tots):
        ia, ja = ti_ref[t], tj_ref[t]
        diag = (ia == ja).astype(jnp.int32)
        z = [lax.dot_general(qb_ref[rows(ia), lanes[hh]], kb_ref[rows(ja), lanes[hh]], NT_DIMS,
                             preferred_element_type=F32) * scale + madd_ref[hh, diag] for hh in heads]
        tc = jnp.maximum(t - 2, 0)
        ic, jc = ti_ref[tc], tj_ref[tc]
        for hh in heads:
            a = jnp.exp(e_ref[hh] + carry_ref[hh])
            acc_ref[hh] += jnp.dot(a.astype(BF16), vb_ref[rows(jc), lanes[hh]], preferred_element_type=F32)
            carry_ref[hh] += tots[hh]
        new_tots = []
        for hh in heads:
            lk = lk_ref[hh]
            after = _split_dot(lk, u)
            e_ref[hh] = ls_ref[hh] + after
            new_tots.append(after[:, 0:1] + lk[:, 0:1])
        for hh in heads:
            sp = _softplus(z[hh])
            ls_ref[hh] = z[hh] - sp
            lk_ref[hh] = -sp

        @pl.when(jnp.logical_and(t >= 2, jc == 0))
        def _():
            for hh in heads:
                o_ref[rows(ic), lanes[hh]] = _rmsnorm(acc_ref[hh], gn_ref[:, lanes[hh]]).astype(o_ref.dtype)
            acc_ref[...] = jnp.zeros(acc_ref.shape, F32)
            carry_ref[...] = jnp.zeros(carry_ref.shape, F32)

        return tuple(new_tots)

    lax.fori_loop(0, n_tiles + 2, body, tuple(jnp.zeros((TQ, 1), F32) for _ in heads))


def _sb_prompt(proj, b_sb, g_heads, layer, B, T):
    TQ = min(256, T)
    nq = T // TQ
    H, d = SB_HEADS, SB_DIM
    NH = 2
    W = NH * d
    HP = H // NH
    tiles = [(i, j) for i in range(nq) for j in range(i, -1, -1)]
    tiles += [tiles[-1]] * 2
    ti = jnp.asarray([t[0] for t in tiles], jnp.int32)
    tj = jnp.asarray([t[1] for t in tiles], jnp.int32)
    smem = pl.BlockSpec(memory_space=pltpu.SMEM)
    return pl.pallas_call(
        functools.partial(_sb_prompt_kernel, TQ=TQ, NH=NH, n_tiles=len(tiles) - 2, layer=layer),
        grid=(B, HP),
        in_specs=[smem, smem, smem,
                  pl.BlockSpec((T, W), lambda b, h: (b, COL_SQ * HP + h)),
                  pl.BlockSpec((T, W), lambda b, h: (b, COL_SK * HP + h)),
                  pl.BlockSpec((T, W), lambda b, h: (b, COL_SV * HP + h)),
                  pl.BlockSpec((None, 1, W), lambda b, h: (layer, 0, ML_WIDTH // W + h))],
        out_specs=pl.BlockSpec((T, W), lambda b, h: (b, h)),
        out_shape=jax.ShapeDtypeStruct((B * T, SB_WIDTH), BF16),
        scratch_shapes=[pltpu.VMEM((T, W), BF16), pltpu.VMEM((T, W), BF16), pltpu.VMEM((T, W), BF16),
                        pltpu.VMEM((NH, 2, TQ, TQ), F32),
                        pltpu.VMEM((NH, TQ, TQ), F32), pltpu.VMEM((NH, TQ, TQ), F32),
                        pltpu.VMEM((NH, TQ, TQ), F32),
                        pltpu.VMEM((NH, TQ, d), F32), pltpu.VMEM((NH, TQ, 1), F32)],
        compiler_params=_params(("parallel", "parallel")),
        name="sb_prompt",
    )(b_sb, ti, tj, proj, proj, proj, g_heads)


def _sb_sample_kernel(pt_ref, q_ref, kn_ref, vn_ref, bias_ref, gn_ref, *rest, T, G):
    k_refs, v_refs = rest[:G], rest[G:2 * G]
    o_ref, acc_ref, carry_ref = rest[2 * G:]
    H, d = SB_HEADS, SB_DIM
    R = H * T
    s_id = pl.program_id(1)
    scale = d ** -0.5
    row = lax.broadcasted_iota(jnp.int32, (PAGE, PAGE), 0)
    col = lax.broadcasted_iota(jnp.int32, (PAGE, PAGE), 1)
    u = (row > col).astype(BF16)
    qb = q_ref[...].astype(BF16)
    bias = bias_ref[...]

    def logits(keys_of_head):
        z = [lax.dot_general(qb[h * T:(h + 1) * T], keys_of_head(h), NT_DIMS, preferred_element_type=F32)
             for h in range(H)]
        return jnp.concatenate(z, axis=0) * scale + bias

    def weighted_values(a, values_of_head):
        ab = a.astype(BF16)
        o = [jnp.dot(ab[h * T:(h + 1) * T], values_of_head(h), preferred_element_type=F32) for h in range(H)]
        return jnp.concatenate(o, axis=0)

    @pl.when(s_id == 0)
    def _():
        zpad = jnp.zeros((PAGE - T, d), F32)
        new_rows = lambda ref: lambda h: jnp.concatenate([ref[h * T:(h + 1) * T, :], zpad], axis=0).astype(BF16)
        z = logits(new_rows(kn_ref))
        mask = (lax.broadcasted_iota(jnp.int32, (R, PAGE), 1) < lax.broadcasted_iota(jnp.int32, (R, PAGE), 0) % T)
        a, carry = _sb_block(z, u, jnp.zeros((R, 1), F32), mask)
        carry_ref[...] = carry
        acc_ref[...] = weighted_values(a, new_rows(vn_ref))

    def page_rows(refs):
        return lambda h: jnp.concatenate([r[pl.ds(h, PAGE, stride=H), :] for r in refs], axis=0).astype(BF16)

    z = logits(page_rows(k_refs))
    sp = _softplus(z)
    lk = -sp
    lk_st = jnp.concatenate([lk[:, g * PAGE:(g + 1) * PAGE] for g in range(G)], axis=0)
    after_st = _split_dot(lk_st, u)
    tot_st = after_st[:, 0:1] + lk_st[:, 0:1]
    carry = carry_ref[...]
    after = []
    for g in range(G):
        after.append(after_st[g * R:(g + 1) * R] + carry)
        carry = carry + tot_st[g * R:(g + 1) * R]
    carry_ref[...] = carry
    a = jnp.exp((z - sp) + jnp.concatenate(after, axis=1))
    acc_ref[...] += weighted_values(a, page_rows(v_refs))

    @pl.when(s_id == pl.num_programs(1) - 1)
    def _():
        o_ref[...] = _rmsnorm(acc_ref[...], gn_ref[...])


def _sb_sample(q, k_new, v_new, cache_k, cache_v, layer, page_table, bias_col, gn_rows, B, T):
    n_pages = page_table.shape[1]
    G = 16 if n_pages % 16 == 0 else (8 if n_pages % 8 == 0 else 1)
    nsteps = n_pages // G
    H, d = SB_HEADS, SB_DIM
    R = H * T

    def page_spec(g):
        return pl.BlockSpec((None, None, PAGE * H, d),
                            lambda b, s, pt: (layer, pt[b, n_pages - 1 - (s * G + g)], 0, 0))

    row_spec = pl.BlockSpec((None, R, d), lambda b, s, pt: (b, 0, 0))
    grid_spec = pltpu.PrefetchScalarGridSpec(
        num_scalar_prefetch=1,
        grid=(B, nsteps),
        in_specs=[row_spec, row_spec, row_spec,
                  pl.BlockSpec((None, R, 1), lambda b, s, pt: (layer, 0, 0)),
                  pl.BlockSpec((None, R, d), lambda b, s, pt: (layer, 0, 0))]
                 + [page_spec(g) for g in range(G)] + [page_spec(g) for g in range(G)],
        out_specs=row_spec,
        scratch_shapes=[pltpu.VMEM((R, d), F32), pltpu.VMEM((R, 1), F32)],
    )
    return pl.pallas_call(
        functools.partial(_sb_sample_kernel, T=T, G=G),
        grid_spec=grid_spec,
        out_shape=jax.ShapeDtypeStruct((B, R, d), F32),
        compiler_params=_params(("parallel", "arbitrary")),
        name="sb_sample",
    )(page_table, q, k_new, v_new, bias_col, gn_rows, *([cache_k] * G), *([cache_v] * G))


def _out_proj_kernel(a1_ref, a2_ref, w1_ref, w2_ref, x_ref, o_ref):
    o_ref[...] = (x_ref[...]
                  + jnp.dot(a1_ref[...].astype(BF16), w1_ref[...], preferred_element_type=F32)
                  + jnp.dot(a2_ref[...].astype(BF16), w2_ref[...], preferred_element_type=F32))


def _out_proj(a1, a2, w, layer, x, tm):
    M, D = x.shape
    K1, K2 = a1.shape[1], a2.shape[1]
    return pl.pallas_call(
        _out_proj_kernel,
        grid=(M // tm,),
        in_specs=[pl.BlockSpec((tm, K1), lambda i: (i, 0)),
                  pl.BlockSpec((tm, K2), lambda i: (i, 0)),
                  pl.BlockSpec((None, K1, D), lambda i: (layer, 0, 0)),
                  pl.BlockSpec((None, K2, D), lambda i: (layer, K1 // K2, 0)),
                  pl.BlockSpec((tm, D), lambda i: (i, 0))],
        out_specs=pl.BlockSpec((tm, D), lambda i: (i, 0)),
        out_shape=jax.ShapeDtypeStruct((M, D), F32),
        compiler_params=_params(("parallel",)),
        name="out_proj",
    )(a1, a2, w, w, x)


def _cross_kernel(x_ref, g_ref, wq_ref, mk_ref, mv_ref, wo_ref, o_ref):
    x = x_ref[...]
    hn = _rmsnorm(x, g_ref[...]).astype(BF16)
    q = jnp.dot(hn, wq_ref[...], preferred_element_type=F32)
    scale = X_DIM ** -0.5
    outs = []
    for h in range(X_HEADS):
        sl = slice(h * X_DIM, (h + 1) * X_DIM)
        s = lax.dot_general(q[:, sl].astype(BF16), mk_ref[:, sl].astype(BF16), NT_DIMS,
                            preferred_element_type=F32) * scale
        e = jnp.exp(s - jnp.max(s, axis=-1, keepdims=True))
        p = e / jnp.sum(e, axis=-1, keepdims=True)
        outs.append(jnp.dot(p.astype(BF16), mv_ref[:, sl].astype(BF16), preferred_element_type=F32))
    o = jnp.concatenate(outs, axis=1).astype(BF16)
    o_ref[...] = x + jnp.dot(o, wo_ref[...], preferred_element_type=F32)


def _cross(x, g, wq, wo, layer, mk, mv, mk_spec, mv_spec, tm):
    M, D = x.shape
    return pl.pallas_call(
        _cross_kernel,
        grid=(M // tm,),
        in_specs=[pl.BlockSpec((tm, D), lambda i: (i, 0)),
                  pl.BlockSpec((None, 1, D), lambda i: (layer, 0, 0)),
                  pl.BlockSpec((None, D, X_WIDTH), lambda i: (layer, 0, 0)),
                  mk_spec, mv_spec,
                  pl.BlockSpec((None, X_WIDTH, D), lambda i: (layer, 0, 0))],
        out_specs=pl.BlockSpec((tm, D), lambda i: (i, 0)),
        out_shape=jax.ShapeDtypeStruct((M, D), F32),
        compiler_params=_params(("parallel",)),
        name="cross",
    )(x, g, wq, mk, mv, wo)


def _ffn_kernel(*refs, tm, from_buf, tiles_per_seq, final):
    refs = list(refs)
    x_ref = refs.pop(0)
    xh_ref = None if from_buf else refs.pop(0)
    g_ref, wa_ref, wb_ref, cw_ref, cb_ref, wd_ref = refs[:6]
    refs = refs[6:]
    buf_ref = refs.pop(0) if from_buf else None
    gf_ref = refs.pop(0) if final else None
    o_ref, st_ref, hn_ref, acc_ref = refs
    HALO = SUBLANES
    f = pl.program_id(1)

    @pl.when(f == 0)
    def _():
        hn_ref[HALO:, :] = _rmsnorm(x_ref[...], g_ref[...]).astype(BF16)
        if not from_buf:
            hn_ref[:HALO, :] = _rmsnorm(xh_ref[...], g_ref[...]).astype(BF16)
        acc_ref[...] = jnp.zeros(acc_ref.shape, F32)

    tf = wa_ref.shape[1]
    if from_buf:
        a = jnp.dot(hn_ref[HALO:, :], wa_ref[...], preferred_element_type=F32)
        nb = buf_ref.shape[1]
        a_ext = jnp.concatenate([buf_ref[j] for j in range(CONV_W - 1)] + [a], axis=0)
        taps = [a_ext[j * nb:j * nb + tm, :] for j in range(CONV_W)]
    else:
        a_ext = jnp.dot(hn_ref[...], wa_ref[...], preferred_element_type=F32)
        seq_start = (pl.program_id(0) % tiles_per_seq) == 0
        rows = lax.broadcasted_iota(jnp.int32, (HALO + tm, 1), 0)
        a_ext = jnp.where(jnp.logical_and(seq_start, rows < HALO), 0.0, a_ext)
        taps = [pltpu.roll(a_ext, 2, axis=0)[HALO:, :], pltpu.roll(a_ext, 1, axis=0)[HALO:, :], a_ext[HALO:, :]]
    b = jnp.dot(hn_ref[HALO:, :], wb_ref[...], preferred_element_type=F32)
    c = cb_ref[...] + ((taps[0] * cw_ref[0:1, :] + taps[1] * cw_ref[1:2, :]) + taps[2] * cw_ref[2:3, :])
    gate = (c * jax.nn.sigmoid(c)) * b
    acc_ref[...] += jnp.dot(gate.astype(BF16), wd_ref[...], preferred_element_type=F32)
    st_ref[...] = taps[2][tm - st_ref.shape[0]:, :]

    @pl.when(f == pl.num_programs(1) - 1)
    def _():
        y = x_ref[...] + acc_ref[...]
        o_ref[...] = _rmsnorm(y, gf_ref[...]) if final else y


def _ffn(x, g, w_up, conv_w, conv_b, w_down, layer, B, T, tm, tf, buf=None, final_gain=None):
    M, D = x.shape
    F = w_down.shape[1]
    nf = F // tf
    from_buf = buf is not None
    assert F % tf == 0 and tm % SUBLANES == 0 and (tm == M and B % SUBLANES == 0 if from_buf else T % tm == 0)
    tiles_per_seq = 1 if from_buf else T // tm
    st_rows = (CONV_W - 1) * B if from_buf else SUBLANES
    hb = tm // SUBLANES
    in_specs = [pl.BlockSpec((tm, D), lambda i, f: (i, 0))]
    args = [x]
    if not from_buf:
        in_specs.append(pl.BlockSpec((SUBLANES, D), lambda i, f: (jnp.maximum(i * hb - 1, 0), 0)))
        args.append(x)
    in_specs += [pl.BlockSpec((None, 1, D), lambda i, f: (layer, 0, 0)),
                 pl.BlockSpec((None, D, tf), lambda i, f: (layer, 0, f)),
                 pl.BlockSpec((None, D, tf), lambda i, f: (layer, 0, nf + f)),
                 pl.BlockSpec((None, CONV_W, tf), lambda i, f: (layer, 0, f)),
                 pl.BlockSpec((None, 1, tf), lambda i, f: (layer, 0, f)),
                 pl.BlockSpec((None, tf, D), lambda i, f: (layer, f, 0))]
    args += [g, w_up, w_up, conv_w, conv_b, w_down]
    if from_buf:
        in_specs.append(pl.BlockSpec((None, CONV_W - 1, B, tf), lambda i, f: (layer, 0, 0, f)))
        args.append(buf)
    if final_gain is not None:
        in_specs.append(pl.BlockSpec((1, D), lambda i, f: (0, 0)))
        args.append(final_gain.reshape(1, D))
    y, st = pl.pallas_call(
        functools.partial(_ffn_kernel, tm=tm, from_buf=from_buf, tiles_per_seq=tiles_per_seq,
                          final=final_gain is not None),
        grid=(M // tm, nf),
        in_specs=in_specs,
        out_specs=[pl.BlockSpec((tm, D), lambda i, f: (i, 0)),
                   pl.BlockSpec((None, st_rows, tf), lambda i, f: (i, 0, f))],
        out_shape=[jax.ShapeDtypeStruct((M, D), F32),
                   jax.ShapeDtypeStruct((M // tm, st_rows, F), F32)],
        scratch_shapes=[pltpu.VMEM((SUBLANES + tm, D), BF16), pltpu.VMEM((tm, D), F32)],
        compiler_params=_params(("parallel", "arbitrary")),
        name="ffn",
    )(*args)
    if from_buf:
        return y, st.reshape(CONV_W - 1, B, F).transpose(1, 0, 2)
    return y, st[tiles_per_seq - 1::tiles_per_seq, SUBLANES - (CONV_W - 1):, :]


def _prep_params(T_s, w_in, b_gates, b_sb, norm_mix, norm_heads, w_out, norm_cross, norm_mem, wx_q, wx_kv, wx_o,
                 norm_ffn, w_up, conv_w, conv_b, w_down):
    depth, D, _ = w_in.shape
    H, d = SB_HEADS, SB_DIM
    g0 = 4 * ML_WIDTH
    g1 = g0 + 2 * ML_HEADS
    pad = LANES - 2 * ML_HEADS
    row = lambda a: a.reshape(depth, 1, -1)
    return dict(
        w_ml=w_in[:, :, :g0].astype(BF16), w_sb=w_in[:, :, g1:].astype(BF16),
        w_gate=jnp.concatenate([w_in[:, :, g0:g1], jnp.zeros((depth, D, pad), F32)], axis=2).astype(BF16),
        bg=row(jnp.concatenate([b_gates, jnp.zeros((depth, pad), F32)], axis=1)),
        w_out=w_out.astype(BF16), wx_q=wx_q.astype(BF16), wx_kv=wx_kv.astype(BF16), wx_o=wx_o.astype(BF16),
        w_up=w_up.astype(BF16), w_down=w_down.astype(BF16),
        norm_mix=row(norm_mix), norm_heads=row(norm_heads), norm_cross=row(norm_cross), norm_mem=row(norm_mem),
        norm_ffn=row(norm_ffn), conv_w=conv_w, conv_b=row(conv_b), b_sb=b_sb,
        bias_col=jnp.repeat(b_sb, T_s, axis=1).reshape(depth, H * T_s, 1),
        gn_rows=jnp.repeat(norm_heads[:, ML_WIDTH:].reshape(depth, H, d), T_s, axis=1))


def _mix_inputs(x, layer, p, tm):
    return _in_proj(x, p["norm_mix"], p["w_ml"], p["w_sb"], p["w_gate"], layer, tm, 1024)


def _prompt_layer(x, mem, B, T, layer, p, final_gain=None):
    n_mem = mem.shape[0] // B
    kv = _norm_matmul(mem, p["norm_mem"], p["wx_kv"], layer, min(512, mem.shape[0]), 512)
    tm = min(512, T)
    proj, gates = _mix_inputs(x, layer, p, min(1024, T))
    mix_ml, c, n, m = _mlstm(proj, gates, p["bg"], p["norm_heads"], layer, B, T, min(LANES, T), None, BF16)
    mix_sb = _sb_prompt(proj, p["b_sb"], p["norm_heads"], layer, B, T)
    x = _out_proj(mix_ml, mix_sb, p["w_out"], layer, x, tm)
    tpb = T // tm
    x = _cross(x, p["norm_cross"], p["wx_q"], p["wx_o"], layer, kv, kv,
               pl.BlockSpec((n_mem, X_WIDTH), lambda i: (i // tpb, 0)),
               pl.BlockSpec((n_mem, X_WIDTH), lambda i: (i // tpb, 1)), tm)
    x, st = _ffn(x, p["norm_ffn"], p["w_up"], p["conv_w"], p["conv_b"], p["w_down"], layer, B, T, tm, 512,
                 final_gain=final_gain)
    return x, (proj, c, n, m, st, kv)


def _sample_layer(x, B, T, layer, p, cache_k, cache_v, page_table, state, conv_state, mem_k, mem_v,
                  final_gain=None):
    M = B * T
    H, d = SB_HEADS, SB_DIM
    proj, gates = _mix_inputs(x, layer, p, M)
    mix_ml, c, n, m = _mlstm(proj, gates, p["bg"], p["norm_heads"], layer, B, T, T, state, F32)
    heads = lambda colblk: proj[:, colblk * SB_WIDTH:(colblk + 1) * SB_WIDTH].reshape(B, T, H, d)
    sbk, sbv = heads(COL_SK), heads(COL_SV)
    head_rows = lambda a: a.transpose(0, 2, 1, 3).reshape(B, H * T, d)
    mix_sb = _sb_sample(head_rows(heads(COL_SQ)), head_rows(sbk), head_rows(sbv), cache_k, cache_v, layer,
                        page_table, p["bias_col"], p["gn_rows"], B, T)
    mix_sb = mix_sb.reshape(B, H, T, d).transpose(0, 2, 1, 3).reshape(M, SB_WIDTH)
    x = _out_proj(mix_ml, mix_sb, p["w_out"], layer, x, M)
    n_mem = mem_k.shape[2]
    mem_spec = pl.BlockSpec((None, None, n_mem, X_WIDTH), lambda i: (layer, i, 0, 0))
    x = _cross(x, p["norm_cross"], p["wx_q"], p["wx_o"], layer, mem_k, mem_v, mem_spec, mem_spec, T)
    time_major = lambda a, n0, n1: a.reshape(n0, n1, -1).transpose(1, 0, 2).reshape(M, -1)
    x, st = _ffn(time_major(x, B, T), p["norm_ffn"], p["w_up"], p["conv_w"], p["conv_b"], p["w_down"], layer,
                 B, T, M, 512, buf=conv_state, final_gain=final_gain)
    x = time_major(x, T, B)
    return x, (sbk, sbv, c, n, m, st)


def kernel(x_prompt, x_sample, mem_prompt, cache_sb_k, cache_sb_v, state_mlstm_c, state_mlstm_n, state_mlstm_m,
           state_ffn_conv, cache_mem_k, cache_mem_v, page_table, norm_mix, w_in, b_gates, b_sb, norm_heads, w_out,
           norm_cross, norm_mem, wx_q, wx_kv, wx_o, norm_ffn, w_up, conv_w, conv_b, w_down, norm_final):
    Bp, Tp, D = x_prompt.shape
    Bs, Ts, _ = x_sample.shape
    depth = w_in.shape[0]
    n_mem = cache_mem_k.shape[2]
    n_pool = cache_sb_k.shape[1]
    p = _prep_params(Ts, w_in, b_gates, b_sb, norm_mix, norm_heads, w_out, norm_cross, norm_mem, wx_q, wx_kv, wx_o,
                     norm_ffn, w_up, conv_w, conv_b, w_down)
    state = (state_mlstm_c, state_mlstm_n, state_mlstm_m)
    conv_state = state_ffn_conv.transpose(0, 2, 1, 3)
    cache_k = cache_sb_k.reshape(depth, n_pool, PAGE * SB_HEADS, SB_DIM)
    cache_v = cache_sb_v.reshape(depth, n_pool, PAGE * SB_HEADS, SB_DIM)
    mem_k = cache_mem_k.reshape(depth, Bs, n_mem, X_WIDTH)
    mem_v = cache_mem_v.reshape(depth, Bs, n_mem, X_WIDTH)
    yp = x_prompt.reshape(Bp * Tp, D)
    ys = x_sample.reshape(Bs * Ts, D)
    mem = mem_prompt.reshape(Bp * mem_prompt.shape[1], D)
    P = [[] for _ in range(6)]
    S = [[] for _ in range(6)]
    for l in range(depth):
        gf = norm_final if l == depth - 1 else None
        yp, st_p = _prompt_layer(yp, mem, Bp, Tp, l, p, gf)
        for lst, a in zip(P, st_p):
            lst.append(a)
        ys, st_s = _sample_layer(ys, Bs, Ts, l, p, cache_k, cache_v, page_table, state, conv_state, mem_k, mem_v,
                                 gf)
        for lst, a in zip(S, st_s):
            lst.append(a)
    y_prompt = yp.reshape(Bp, Tp, D)
    y_sample = ys.reshape(Bs, Ts, D)
    projs, pc, pn, pm, pconv, kvs = P
    cols = lambda a, j: jnp.stack([x[:, j * SB_WIDTH:(j + 1) * SB_WIDTH] for x in a])
    p_sb_k = cols(projs, COL_SK).reshape(depth, Bp, Tp, SB_HEADS, SB_DIM)
    p_sb_v = cols(projs, COL_SV).reshape(depth, Bp, Tp, SB_HEADS, SB_DIM)
    kv = jnp.stack(kvs)
    p_mem_k = kv[:, :, :X_WIDTH].reshape(depth, Bp, n_mem, X_HEADS, X_DIM)
    p_mem_v = kv[:, :, X_WIDTH:].reshape(depth, Bp, n_mem, X_HEADS, X_DIM)
    return ((y_prompt, y_sample, p_sb_k, p_sb_v, jnp.stack(pc), jnp.stack(pn), jnp.stack(pm), jnp.stack(pconv),
             p_mem_k, p_mem_v) + tuple(jnp.stack(a) for a in S))
```

```python
import functools

import jax
import jax.numpy as jnp
from jax import lax
from jax.experimental import pallas as pl
from jax.experimental.pallas import tpu as pltpu

F32 = jnp.float32
BF16 = jnp.bfloat16

EPS = 1e-6
M_EMPTY = -1e30
ML_HEADS = 4
ML_DIM = 256
SB_HEADS = 8
SB_DIM = 128
X_HEADS = 4
X_DIM = 128
PAGE = 128
CONV_W = 3
ML_WIDTH = ML_HEADS * ML_DIM
SB_WIDTH = SB_HEADS * SB_DIM
X_WIDTH = X_HEADS * X_DIM

SUBLANES = 8
LANES = 128
MIB = 1024 * 1024
VMEM_LIMIT = 56 * MIB

COL_MQ, COL_MK, COL_MV, COL_MO = 0, 1, 2, 3
COL_SQ, COL_SK, COL_SV = 4, 5, 6
N_PROJ = 7 * ML_WIDTH

NT_DIMS = (((1,), (1,)), ((), ()))
TN_DIMS = (((0,), (0,)), ((), ()))


def _params(sem):
    return pltpu.CompilerParams(dimension_semantics=sem, vmem_limit_bytes=VMEM_LIMIT)


def _rmsnorm(xf, g):
    return xf * lax.rsqrt(jnp.mean(xf * xf, axis=-1, keepdims=True) + EPS) * g


def _log_sigmoid(x):
    return jnp.minimum(x, 0.0) - jnp.log1p(jnp.exp(-jnp.abs(x)))


def _softplus(z):
    t = jnp.exp(-jnp.abs(z))
    u = 1.0 + t
    return jnp.maximum(z, 0.0) + (jnp.log(u) + (t - (u - 1.0)))


def _split_dot(x, u):
    hi = x.astype(BF16)
    lo = (x - hi.astype(F32)).astype(BF16)
    return (jnp.dot(hi, u, preferred_element_type=F32) + jnp.dot(lo, u, preferred_element_type=F32))


def _norm_matmul_kernel(x_ref, g_ref, w_ref, o_ref, hn_ref):
    @pl.when(pl.program_id(1) == 0)
    def _():
        hn_ref[...] = _rmsnorm(x_ref[...], g_ref[...]).astype(BF16)

    o_ref[...] = jnp.dot(hn_ref[...], w_ref[...], preferred_element_type=F32)


def _norm_matmul(x, g, w, layer, tm, tn):
    M, D = x.shape
    N = w.shape[2]
    assert M % tm == 0 and N % tn == 0
    return pl.pallas_call(
        _norm_matmul_kernel,
        grid=(M // tm, N // tn),
        in_specs=[pl.BlockSpec((tm, D), lambda i, j: (i, 0)),
                  pl.BlockSpec((None, 1, D), lambda i, j: (layer, 0, 0)),
                  pl.BlockSpec((None, D, tn), lambda i, j: (layer, 0, j))],
        out_specs=pl.BlockSpec((tm, tn), lambda i, j: (i, j)),
        out_shape=jax.ShapeDtypeStruct((M, N), F32),
        scratch_shapes=[pltpu.VMEM((tm, D), BF16)],
        compiler_params=_params(("parallel", "arbitrary")),
        name="norm_matmul",
    )(x, g, w)


def _in_proj_kernel(x_ref, g_ref, wml_ref, wsb_ref, wg_ref, o_ref, og_ref, hn_ref, *, n_ml):
    n = pl.program_id(1)

    @pl.when(n == 0)
    def _():
        hn_ref[...] = _rmsnorm(x_ref[...], g_ref[...]).astype(BF16)
        og_ref[...] = jnp.dot(hn_ref[...], wg_ref[...], preferred_element_type=F32)

    @pl.when(n < n_ml)
    def _():
        o_ref[...] = jnp.dot(hn_ref[...], wml_ref[...], preferred_element_type=F32)

    @pl.when(n >= n_ml)
    def _():
        o_ref[...] = jnp.dot(hn_ref[...], wsb_ref[...], preferred_element_type=F32)


def _in_proj(x, g, w_ml, w_sb, w_gate, layer, tm, tn):
    M, D = x.shape
    n_ml, n_sb = w_ml.shape[2] // tn, w_sb.shape[2] // tn
    assert M % tm == 0 and w_ml.shape[2] % tn == 0 and w_sb.shape[2] % tn == 0
    return pl.pallas_call(
        functools.partial(_in_proj_kernel, n_ml=n_ml),
        grid=(M // tm, n_ml + n_sb),
        in_specs=[pl.BlockSpec((tm, D), lambda i, j: (i, 0)),
                  pl.BlockSpec((None, 1, D), lambda i, j: (layer, 0, 0)),
                  pl.BlockSpec((None, D, tn), lambda i, j: (layer, 0, jnp.minimum(j, n_ml - 1))),
                  pl.BlockSpec((None, D, tn), lambda i, j: (layer, 0, jnp.maximum(j - n_ml, 0))),
                  pl.BlockSpec((None, D, LANES), lambda i, j: (layer, 0, 0))],
        out_specs=[pl.BlockSpec((tm, tn), lambda i, j: (i, j)),
                   pl.BlockSpec((tm, LANES), lambda i, j: (i, 0))],
        out_shape=[jax.ShapeDtypeStruct((M, (n_ml + n_sb) * tn), F32),
                   jax.ShapeDtypeStruct((M, LANES), F32)],
        scratch_shapes=[pltpu.VMEM((tm, D), BF16)],
        compiler_params=_params(("parallel", "arbitrary")),
        name="in_proj",
    )(x, g, w_ml, w_sb, w_gate)


def _mlstm_kernel(q_ref, k_ref, v_ref, o_ref, gt_ref, bg_ref, gn_ref, *rest, L, has_init):
    if has_init:
        c0_ref, n0_ref, m0_ref, mix_ref, c_ref, n_ref, m_ref = rest
    else:
        mix_ref, c_ref, n_ref, m_ref = rest
    S = LANES
    H, d = ML_HEADS, ML_DIM

    @pl.when(pl.program_id(1) == 0)
    def _():
        if has_init:
            c_ref[...] = c0_ref[...]
            n_ref[...] = n0_ref[...]
            m_ref[...] = m0_ref[...]
        else:
            c_ref[...] = jnp.zeros(c_ref.shape, F32)
            n_ref[...] = jnp.zeros(n_ref.shape, F32)
            m_ref[...] = jnp.full(m_ref.shape, M_EMPTY, F32)

    def pad(a):
        if L == S:
            return a
        return jnp.concatenate([a, jnp.zeros((S - L, a.shape[1]), a.dtype)], axis=0)

    gates = pad(gt_ref[...] + bg_ref[...])
    row = lax.broadcasted_iota(jnp.int32, (S, S), 0)
    col = lax.broadcasted_iota(jnp.int32, (S, S), 1)
    causal = col <= row
    valid = lax.broadcasted_iota(jnp.int32, (S, 1), 0) < L
    b_all = jnp.dot(causal.astype(F32), _log_sigmoid(gates), precision=lax.Precision.HIGHEST,
                    preferred_element_type=F32)
    r_t = (gates - pltpu.roll(b_all, LANES - H, axis=1)).T

    for h in range(H):
        hs = slice(h * d, (h + 1) * d)
        i_col = gates[:, h:h + 1]
        b_col = b_all[:, H + h:H + h + 1]
        g_col = b_col + m_ref[h]
        logd = b_col + r_t[h:h + 1, :]
        mt = jnp.maximum(g_col, jnp.max(jnp.where(causal, logd, -jnp.inf), axis=-1, keepdims=True))

        q = pad(q_ref[:, hs])
        k = pad(k_ref[:, hs]) * (d ** -0.5)
        v = pad(v_ref[:, hs])
        qb, kb = q.astype(BF16), k.astype(BF16)
        s = lax.dot_general(q, k, NT_DIMS, precision=lax.Precision.HIGHEST, preferred_element_type=F32)
        s = jnp.where(causal, s * jnp.exp(logd - mt), 0.0)
        gw = jnp.exp(g_col - mt)
        C = c_ref[0, h]
        n_row = n_ref[h]
        num = (jnp.dot(s.astype(BF16), v.astype(BF16), preferred_element_type=F32)
               + gw * jnp.dot(qb, C.astype(BF16), preferred_element_type=F32))
        den = jnp.sum(s, axis=-1, keepdims=True) + gw * jnp.sum(q * n_row, axis=-1, keepdims=True)
        den = jnp.maximum(jnp.abs(den), jnp.exp(-mt))
        hcell = num / den

        m_new = mt[L - 1:L, :]
        wk = jnp.where(valid, jnp.exp(b_col[L - 1:L, :] - b_col + i_col - m_new), 0.0)
        decay = jnp.exp(g_col[L - 1:L, :] - m_new)
        c_ref[0, h] = decay * C + lax.dot_general(kb, (wk * v).astype(BF16), TN_DIMS,
                                                  preferred_element_type=F32)
        n_ref[h] = decay * n_row + jnp.sum(wk * k, axis=0, keepdims=True)
        m_ref[h] = m_new

        hm = jax.nn.sigmoid(o_ref[:, hs]) * hcell[:L]
        mix_ref[:, hs] = _rmsnorm(hm, gn_ref[:, hs]).astype(mix_ref.dtype)


def _mlstm(proj, gates, b_gates, g_heads, layer, B, T, L, init, mix_dtype):
    nc = T // L
    H, d = ML_HEADS, ML_DIM
    W = ML_WIDTH
    has_init = init is not None

    def qkvo(colblk):
        return pl.BlockSpec((L, W), lambda b, c: (b * nc + c, colblk))

    state_specs = [pl.BlockSpec((1, H, d, d), lambda b, c: (b, 0, 0, 0)),
                   pl.BlockSpec((H, 1, d), lambda b, c: (b, 0, 0)),
                   pl.BlockSpec((H, 1, 1), lambda b, c: (b, 0, 0))]
    in_specs = [qkvo(COL_MQ), qkvo(COL_MK), qkvo(COL_MV), qkvo(COL_MO),
                pl.BlockSpec((L, LANES), lambda b, c: (b * nc + c, 0)),
                pl.BlockSpec((None, 1, LANES), lambda b, c: (layer, 0, 0)),
                pl.BlockSpec((None, 1, W), lambda b, c: (layer, 0, 0))]
    args = [proj, proj, proj, proj, gates, b_gates, g_heads]
    if has_init:
        c0, n0, m0 = init
        depth = c0.shape[0]
        in_specs += [pl.BlockSpec((None, 1, H, d, d), lambda b, c: (layer, b, 0, 0, 0)),
                     pl.BlockSpec((None, H, 1, d), lambda b, c: (layer, b, 0, 0)),
                     pl.BlockSpec((None, H, 1, 1), lambda b, c: (layer, b, 0, 0))]
        args += [c0, n0.reshape(depth, B * H, 1, d), m0.reshape(depth, B * H, 1, 1)]
    mix, c, n, m = pl.pallas_call(
        functools.partial(_mlstm_kernel, L=L, has_init=has_init),
        grid=(B, nc),
        in_specs=in_specs,
        out_specs=[pl.BlockSpec((L, W), lambda b, c: (b * nc + c, 0))] + state_specs,
        out_shape=[jax.ShapeDtypeStruct((B * T, W), mix_dtype),
                   jax.ShapeDtypeStruct((B, H, d, d), F32),
                   jax.ShapeDtypeStruct((B * H, 1, d), F32),
                   jax.ShapeDtypeStruct((B * H, 1, 1), F32)],
        compiler_params=_params(("parallel", "arbitrary")),
        name="mlstm",
    )(*args)
    return mix, c, n.reshape(B, H, d), m.reshape(B, H)


def _sb_block(z, u, carry, mask):
    sp = _softplus(z)
    lk = -sp
    if mask is not None:
        lk = jnp.where(mask, lk, 0.0)
    after = _split_dot(lk, u)
    a = jnp.exp((z - sp) + after + carry)
    if mask is not None:
        a = jnp.where(mask, a, 0.0)
    return a, carry + (after[:, 0:1] + lk[:, 0:1])


def _sb_prompt_kernel(bias_ref, ti_ref, tj_ref, q_ref, k_ref, v_ref, gn_ref, o_ref,
                      qb_ref, kb_ref, vb_ref, madd_ref, ls_ref, lk_ref, e_ref, acc_ref, carry_ref, *,
                      TQ, NH, n_tiles, layer):
    d = SB_DIM
    scale = d ** -0.5
    row = lax.broadcasted_iota(jnp.int32, (TQ, TQ), 0)
    col = lax.broadcasted_iota(jnp.int32, (TQ, TQ), 1)
    u = (row > col).astype(BF16)
    heads = range(NH)
    lanes = [slice(hh * d, (hh + 1) * d) for hh in heads]

    qb_ref[...] = q_ref[...].astype(BF16)
    kb_ref[...] = k_ref[...].astype(BF16)
    vb_ref[...] = v_ref[...].astype(BF16)
    for hh in heads:
        bias = bias_ref[layer, pl.program_id(1) * NH + hh]
        madd_ref[hh, 0] = jnp.full((TQ, TQ), bias, F32)
        madd_ref[hh, 1] = jnp.where(col < row, bias, M_EMPTY)
    ls_ref[...] = jnp.full(ls_ref.shape, M_EMPTY, F32)
    lk_ref[...] = jnp.zeros(lk_ref.shape, F32)
    e_ref[...] = jnp.full(e_ref.shape, M_EMPTY, F32)
    acc_ref[...] = jnp.zeros(acc_ref.shape, F32)
    carry_ref[...] = jnp.zeros(carry_ref.shape, F32)

    def rows(blk):
        return pl.ds(pl.multiple_of(blk * TQ, TQ), TQ)

    def body(t, tots):
        ia, ja = ti_ref[t], tj_ref[t]
        diag = (ia == ja).astype(jnp.int32)
        z = [lax.dot_general(qb_ref[rows(ia), lanes[hh]], kb_ref[rows(ja), lanes[hh]], NT_DIMS,
                             preferred_element_type=F32) * scale + madd_ref[hh, diag] for hh in heads]
        tc = jnp.maximum(t - 2, 0)
        ic, jc = ti_ref[tc], tj_ref[tc]
        for hh in heads:
            a = jnp.exp(e_ref[hh] + carry_ref[hh])
            acc_ref[hh] += jnp.dot(a.astype(BF16), vb_ref[rows(jc), lanes[hh]], preferred_element_type=F32)
            carry_ref[hh] += tots[hh]
        new_tots = []
        for hh in heads:
            lk = lk_ref[hh]
            after = _split_dot(lk, u)
            e_ref[hh] = ls_ref[hh] + after
            new_tots.append(after[:, 0:1] + lk[:, 0:1])
        for hh in heads:
            sp = _softplus(z[hh])
            ls_ref[hh] = z[hh] - sp
            lk_ref[hh] = -sp

        @pl.when(jnp.logical_and(t >= 2, jc == 0))
        def _():
            for hh in heads:
                o_ref[rows(ic), lanes[hh]] = _rmsnorm(acc_ref[hh], gn_ref[:, lanes[hh]]).astype(o_ref.dtype)
            acc_ref[...] = jnp.zeros(acc_ref.shape, F32)
            carry_ref[...] = jnp.zeros(carry_ref.shape, F32)

        return tuple(new_tots)

    lax.fori_loop(0, n_tiles + 2, body, tuple(jnp.zeros((TQ, 1), F32) for _ in heads))


def _sb_prompt(proj, b_sb, g_heads, layer, B, T):
    TQ = min(256, T)
    nq = T // TQ
    H, d = SB_HEADS, SB_DIM
    NH = 2
    W = NH * d
    HP = H // NH
    tiles = [(i, j) for i in range(nq) for j in range(i, -1, -1)]
    tiles += [tiles[-1]] * 2
    ti = jnp.asarray([t[0] for t in tiles], jnp.int32)
    tj = jnp.asarray([t[1] for t in tiles], jnp.int32)
    smem = pl.BlockSpec(memory_space=pltpu.SMEM)
    return pl.pallas_call(
        functools.partial(_sb_prompt_kernel, TQ=TQ, NH=NH, n_tiles=len(tiles) - 2, layer=layer),
        grid=(B, HP),
        in_specs=[smem, smem, smem,
                  pl.BlockSpec((T, W), lambda b, h: (b, COL_SQ * HP + h)),
                  pl.BlockSpec((T, W), lambda b, h: (b, COL_SK * HP + h)),
                  pl.BlockSpec((T, W), lambda b, h: (b, COL_SV * HP + h)),
                  pl.BlockSpec((None, 1, W), lambda b, h: (layer, 0, ML_WIDTH // W + h))],
        out_specs=pl.BlockSpec((T, W), lambda b, h: (b, h)),
        out_shape=jax.ShapeDtypeStruct((B * T, SB_WIDTH), BF16),
        scratch_shapes=[pltpu.VMEM((T, W), BF16), pltpu.VMEM((T, W), BF16), pltpu.VMEM((T, W), BF16),
                        pltpu.VMEM((NH, 2, TQ, TQ), F32),
                        pltpu.VMEM((NH, TQ, TQ), F32), pltpu.VMEM((NH, TQ, TQ), F32),
                        pltpu.VMEM((NH, TQ, TQ), F32),
                        pltpu.VMEM((NH, TQ, d), F32), pltpu.VMEM((NH, TQ, 1), F32)],
        compiler_params=_params(("parallel", "parallel")),
        name="sb_prompt",
    )(b_sb, ti, tj, proj, proj, proj, g_heads)


def _sb_sample_kernel(pt_ref, q_ref, kn_ref, vn_ref, bias_ref, gn_ref, *rest, T, G):
    k_refs, v_refs = rest[:G], rest[G:2 * G]
    o_ref, acc_ref, carry_ref = rest[2 * G:]
    H, d = SB_HEADS, SB_DIM
    R = H * T
    s_id = pl.program_id(1)
    scale = d ** -0.5
    row = lax.broadcasted_iota(jnp.int32, (PAGE, PAGE), 0)
    col = lax.broadcasted_iota(jnp.int32, (PAGE, PAGE), 1)
    u = (row > col).astype(BF16)
    qb = q_ref[...].astype(BF16)
    bias = bias_ref[...]

    def logits(keys_of_head):
        z = [lax.dot_general(qb[h * T:(h + 1) * T], keys_of_head(h), NT_DIMS, preferred_element_type=F32)
             for h in range(H)]
        return jnp.concatenate(z, axis=0) * scale + bias

    def weighted_values(a, values_of_head):
        ab = a.astype(BF16)
        o = [jnp.dot(ab[h * T:(h + 1) * T], values_of_head(h), preferred_element_type=F32) for h in range(H)]
        return jnp.concatenate(o, axis=0)

    @pl.when(s_id == 0)
    def _():
        zpad = jnp.zeros((PAGE - T, d), F32)
        new_rows = lambda ref: lambda h: jnp.concatenate([ref[h * T:(h + 1) * T, :], zpad], axis=0).astype(BF16)
        z = logits(new_rows(kn_ref))
        mask = (lax.broadcasted_iota(jnp.int32, (R, PAGE), 1) < lax.broadcasted_iota(jnp.int32, (R, PAGE), 0) % T)
        a, carry = _sb_block(z, u, jnp.zeros((R, 1), F32), mask)
        carry_ref[...] = carry
        acc_ref[...] = weighted_values(a, new_rows(vn_ref))

    def page_rows(refs):
        return lambda h: jnp.concatenate([r[pl.ds(h, PAGE, stride=H), :] for r in refs], axis=0).astype(BF16)

    z = logits(page_rows(k_refs))
    sp = _softplus(z)
    lk = -sp
    lk_st = jnp.concatenate([lk[:, g * PAGE:(g + 1) * PAGE] for g in range(G)], axis=0)
    after_st = _split_dot(lk_st, u)
    tot_st = after_st[:, 0:1] + lk_st[:, 0:1]
    carry = carry_ref[...]
    after = []
    for g in range(G):
        after.append(after_st[g * R:(g + 1) * R] + carry)
        carry = carry + tot_st[g * R:(g + 1) * R]
    carry_ref[...] = carry
    a = jnp.exp((z - sp) + jnp.concatenate(after, axis=1))
    acc_ref[...] += weighted_values(a, page_rows(v_refs))

    @pl.when(s_id == pl.num_programs(1) - 1)
    def _():
        o_ref[...] = _rmsnorm(acc_ref[...], gn_ref[...])


def _sb_sample(q, k_new, v_new, cache_k, cache_v, layer, page_table, bias_col, gn_rows, B, T):
    n_pages = page_table.shape[1]
    G = 16 if n_pages % 16 == 0 else (8 if n_pages % 8 == 0 else 1)
    nsteps = n_pages // G
    H, d = SB_HEADS, SB_DIM
    R = H * T

    def page_spec(g):
        return pl.BlockSpec((None, None, PAGE * H, d),
                            lambda b, s, pt: (layer, pt[b, n_pages - 1 - (s * G + g)], 0, 0))

    row_spec = pl.BlockSpec((None, R, d), lambda b, s, pt: (b, 0, 0))
    grid_spec = pltpu.PrefetchScalarGridSpec(
        num_scalar_prefetch=1,
        grid=(B, nsteps),
        in_specs=[row_spec, row_spec, row_spec,
                  pl.BlockSpec((None, R, 1), lambda b, s, pt: (layer, 0, 0)),
                  pl.BlockSpec((None, R, d), lambda b, s, pt: (layer, 0, 0))]
                 + [page_spec(g) for g in range(G)] + [page_spec(g) for g in range(G)],
        out_specs=row_spec,
        scratch_shapes=[pltpu.VMEM((R, d), F32), pltpu.VMEM((R, 1), F32)],
    )
    return pl.pallas_call(
        functools.partial(_sb_sample_kernel, T=T, G=G),
        grid_spec=grid_spec,
        out_shape=jax.ShapeDtypeStruct((B, R, d), F32),
        compiler_params=_params(("parallel", "arbitrary")),
        name="sb_sample",
    )(page_table, q, k_new, v_new, bias_col, gn_rows, *([cache_k] * G), *([cache_v] * G))


def _out_proj_kernel(a1_ref, a2_ref, w1_ref, w2_ref, x_ref, o_ref):
    o_ref[...] = (x_ref[...]
                  + jnp.dot(a1_ref[...].astype(BF16), w1_ref[...], preferred_element_type=F32)
                  + jnp.dot(a2_ref[...].astype(BF16), w2_ref[...], preferred_element_type=F32))


def _out_proj(a1, a2, w, layer, x, tm):
    M, D = x.shape
    K1, K2 = a1.shape[1], a2.shape[1]
    return pl.pallas_call(
        _out_proj_kernel,
        grid=(M // tm,),
        in_specs=[pl.BlockSpec((tm, K1), lambda i: (i, 0)),
                  pl.BlockSpec((tm, K2), lambda i: (i, 0)),
                  pl.BlockSpec((None, K1, D), lambda i: (layer, 0, 0)),
                  pl.BlockSpec((None, K2, D), lambda i: (layer, K1 // K2, 0)),
                  pl.BlockSpec((tm, D), lambda i: (i, 0))],
        out_specs=pl.BlockSpec((tm, D), lambda i: (i, 0)),
        out_shape=jax.ShapeDtypeStruct((M, D), F32),
        compiler_params=_params(("parallel",)),
        name="out_proj",
    )(a1, a2, w, w, x)


def _cross_kernel(x_ref, g_ref, wq_ref, mk_ref, mv_ref, wo_ref, o_ref):
    x = x_ref[...]
    hn = _rmsnorm(x, g_ref[...]).astype(BF16)
    q = jnp.dot(hn, wq_ref[...], preferred_element_type=F32)
    scale = X_DIM ** -0.5
    outs = []
    for h in range(X_HEADS):
        sl = slice(h * X_DIM, (h + 1) * X_DIM)
        s = lax.dot_general(q[:, sl].astype(BF16), mk_ref[:, sl].astype(BF16), NT_DIMS,
                            preferred_element_type=F32) * scale
        e = jnp.exp(s - jnp.max(s, axis=-1, keepdims=True))
        p = e / jnp.sum(e, axis=-1, keepdims=True)
        outs.append(jnp.dot(p.astype(BF16), mv_ref[:, sl].astype(BF16), preferred_element_type=F32))
    o = jnp.concatenate(outs, axis=1).astype(BF16)
    o_ref[...] = x + jnp.dot(o, wo_ref[...], preferred_element_type=F32)


def _mix_cross_kernel(a1_ref, a2_ref, w1_ref, w2_ref, x_ref, g_ref, wq_ref, mk_ref, mv_ref, wo_ref, o_ref, x1_ref):
    x1_ref[...] = (x_ref[...]
                   + jnp.dot(a1_ref[...].astype(BF16), w1_ref[...], preferred_element_type=F32)
                   + jnp.dot(a2_ref[...].astype(BF16), w2_ref[...], preferred_element_type=F32))
    _cross_kernel(x1_ref, g_ref, wq_ref, mk_ref, mv_ref, wo_ref, o_ref)


def _mix_cross(a1, a2, w_out, x, g, wq, wo, layer, mk, mv, mk_spec, mv_spec, tm):
    M, D = x.shape
    K1, K2 = a1.shape[1], a2.shape[1]
    return pl.pallas_call(
        _mix_cross_kernel,
        grid=(M // tm,),
        in_specs=[pl.BlockSpec((tm, K1), lambda i: (i, 0)),
                  pl.BlockSpec((tm, K2), lambda i: (i, 0)),
                  pl.BlockSpec((None, K1, D), lambda i: (layer, 0, 0)),
                  pl.BlockSpec((None, K2, D), lambda i: (layer, K1 // K2, 0)),
                  pl.BlockSpec((tm, D), lambda i: (i, 0)),
                  pl.BlockSpec((None, 1, D), lambda i: (layer, 0, 0)),
                  pl.BlockSpec((None, D, X_WIDTH), lambda i: (layer, 0, 0)),
                  mk_spec, mv_spec,
                  pl.BlockSpec((None, X_WIDTH, D), lambda i: (layer, 0, 0))],
        out_specs=pl.BlockSpec((tm, D), lambda i: (i, 0)),
        out_shape=jax.ShapeDtypeStruct((M, D), F32),
        scratch_shapes=[pltpu.VMEM((tm, D), F32)],
        compiler_params=_params(("parallel",)),
        name="mix_cross",
    )(a1, a2, w_out, w_out, x, g, wq, mk, mv, wo)


def _cross(x, g, wq, wo, layer, mk, mv, mk_spec, mv_spec, tm):
    M, D = x.shape
    return pl.pallas_call(
        _cross_kernel,
        grid=(M // tm,),
        in_specs=[pl.BlockSpec((tm, D), lambda i: (i, 0)),
                  pl.BlockSpec((None, 1, D), lambda i: (layer, 0, 0)),
                  pl.BlockSpec((None, D, X_WIDTH), lambda i: (layer, 0, 0)),
                  mk_spec, mv_spec,
                  pl.BlockSpec((None, X_WIDTH, D), lambda i: (layer, 0, 0))],
        out_specs=pl.BlockSpec((tm, D), lambda i: (i, 0)),
        out_shape=jax.ShapeDtypeStruct((M, D), F32),
        compiler_params=_params(("parallel",)),
        name="cross",
    )(x, g, wq, mk, mv, wo)


def _ffn_kernel(*refs, tm, from_buf, tiles_per_seq, final):
    refs = list(refs)
    x_ref = refs.pop(0)
    xh_ref = None if from_buf else refs.pop(0)
    g_ref, wa_ref, wb_ref, cw_ref, cb_ref, wd_ref = refs[:6]
    refs = refs[6:]
    buf_ref = refs.pop(0) if from_buf else None
    gf_ref = refs.pop(0) if final else None
    o_ref, st_ref, hn_ref, acc_ref = refs
    HALO = SUBLANES
    f = pl.program_id(1)

    @pl.when(f == 0)
    def _():
        hn_ref[HALO:, :] = _rmsnorm(x_ref[...], g_ref[...]).astype(BF16)
        if not from_buf:
            hn_ref[:HALO, :] = _rmsnorm(xh_ref[...], g_ref[...]).astype(BF16)
        acc_ref[...] = jnp.zeros(acc_ref.shape, F32)

    tf = wa_ref.shape[1]
    if from_buf:
        a = jnp.dot(hn_ref[HALO:, :], wa_ref[...], preferred_element_type=F32)
        nb = buf_ref.shape[1]
        a_ext = jnp.concatenate([buf_ref[j] for j in range(CONV_W - 1)] + [a], axis=0)
        taps = [a_ext[j * nb:j * nb + tm, :] for j in range(CONV_W)]
    else:
        a_ext = jnp.dot(hn_ref[...], wa_ref[...], preferred_element_type=F32)
        seq_start = (pl.program_id(0) % tiles_per_seq) == 0
        rows = lax.broadcasted_iota(jnp.int32, (HALO + tm, 1), 0)
        a_ext = jnp.where(jnp.logical_and(seq_start, rows < HALO), 0.0, a_ext)
        taps = [pltpu.roll(a_ext, 2, axis=0)[HALO:, :], pltpu.roll(a_ext, 1, axis=0)[HALO:, :], a_ext[HALO:, :]]
    b = jnp.dot(hn_ref[HALO:, :], wb_ref[...], preferred_element_type=F32)
    c = cb_ref[...] + ((taps[0] * cw_ref[0:1, :] + taps[1] * cw_ref[1:2, :]) + taps[2] * cw_ref[2:3, :])
    gate = (c * jax.nn.sigmoid(c)) * b
    acc_ref[...] += jnp.dot(gate.astype(BF16), wd_ref[...], preferred_element_type=F32)
    st_ref[...] = taps[2][tm - st_ref.shape[0]:, :]

    @pl.when(f == pl.num_programs(1) - 1)
    def _():
        y = x_ref[...] + acc_ref[...]
        o_ref[...] = _rmsnorm(y, gf_ref[...]) if final else y


def _ffn(x, g, w_up, conv_w, conv_b, w_down, layer, B, T, tm, tf, buf=None, final_gain=None):
    M, D = x.shape
    F = w_down.shape[1]
    nf = F // tf
    from_buf = buf is not None
    assert F % tf == 0 and tm % SUBLANES == 0 and (tm == M and B % SUBLANES == 0 if from_buf else T % tm == 0)
    tiles_per_seq = 1 if from_buf else T // tm
    st_rows = (CONV_W - 1) * B if from_buf else SUBLANES
    hb = tm // SUBLANES
    in_specs = [pl.BlockSpec((tm, D), lambda i, f: (i, 0))]
    args = [x]
    if not from_buf:
        in_specs.append(pl.BlockSpec((SUBLANES, D), lambda i, f: (jnp.maximum(i * hb - 1, 0), 0)))
        args.append(x)
    in_specs += [pl.BlockSpec((None, 1, D), lambda i, f: (layer, 0, 0)),
                 pl.BlockSpec((None, D, tf), lambda i, f: (layer, 0, f)),
                 pl.BlockSpec((None, D, tf), lambda i, f: (layer, 0, nf + f)),
                 pl.BlockSpec((None, CONV_W, tf), lambda i, f: (layer, 0, f)),
                 pl.BlockSpec((None, 1, tf), lambda i, f: (layer, 0, f)),
                 pl.BlockSpec((None, tf, D), lambda i, f: (layer, f, 0))]
    args += [g, w_up, w_up, conv_w, conv_b, w_down]
    if from_buf:
        in_specs.append(pl.BlockSpec((None, CONV_W - 1, B, tf), lambda i, f: (layer, 0, 0, f)))
        args.append(buf)
    if final_gain is not None:
        in_specs.append(pl.BlockSpec((1, D), lambda i, f: (0, 0)))
        args.append(final_gain.reshape(1, D))
    y, st = pl.pallas_call(
        functools.partial(_ffn_kernel, tm=tm, from_buf=from_buf, tiles_per_seq=tiles_per_seq,
                          final=final_gain is not None),
        grid=(M // tm, nf),
        in_specs=in_specs,
        out_specs=[pl.BlockSpec((tm, D), lambda i, f: (i, 0)),
                   pl.BlockSpec((None, st_rows, tf), lambda i, f: (i, 0, f))],
        out_shape=[jax.ShapeDtypeStruct((M, D), F32),
                   jax.ShapeDtypeStruct((M // tm, st_rows, F), F32)],
        scratch_shapes=[pltpu.VMEM((SUBLANES + tm, D), BF16), pltpu.VMEM((tm, D), F32)],
        compiler_params=_params(("parallel", "arbitrary")),
        name="ffn",
    )(*args)
    if from_buf:
        return y, st.reshape(CONV_W - 1, B, F).transpose(1, 0, 2)
    return y, st[tiles_per_seq - 1::tiles_per_seq, SUBLANES - (CONV_W - 1):, :]


def _prep_params(T_s, w_in, b_gates, b_sb, norm_mix, norm_heads, w_out, norm_cross, norm_mem, wx_q, wx_kv, wx_o,
                 norm_ffn, w_up, conv_w, conv_b, w_down):
    depth, D, _ = w_in.shape
    H, d = SB_HEADS, SB_DIM
    g0 = 4 * ML_WIDTH
    g1 = g0 + 2 * ML_HEADS
    pad = LANES - 2 * ML_HEADS
    row = lambda a: a.reshape(depth, 1, -1)
    return dict(
        w_ml=w_in[:, :, :g0].astype(BF16), w_sb=w_in[:, :, g1:].astype(BF16),
        w_gate=jnp.concatenate([w_in[:, :, g0:g1], jnp.zeros((depth, D, pad), F32)], axis=2).astype(BF16),
        bg=row(jnp.concatenate([b_gates, jnp.zeros((depth, pad), F32)], axis=1)),
        w_out=w_out.astype(BF16), wx_q=wx_q.astype(BF16), wx_kv=wx_kv.astype(BF16), wx_o=wx_o.astype(BF16),
        w_up=w_up.astype(BF16), w_down=w_down.astype(BF16),
        norm_mix=row(norm_mix), norm_heads=row(norm_heads), norm_cross=row(norm_cross), norm_mem=row(norm_mem),
        norm_ffn=row(norm_ffn), conv_w=conv_w, conv_b=row(conv_b), b_sb=b_sb,
        bias_col=jnp.repeat(b_sb, T_s, axis=1).reshape(depth, H * T_s, 1),
        gn_rows=jnp.repeat(norm_heads[:, ML_WIDTH:].reshape(depth, H, d), T_s, axis=1))


def _mix_inputs(x, layer, p, tm):
    return _in_proj(x, p["norm_mix"], p["w_ml"], p["w_sb"], p["w_gate"], layer, tm, 1024)


def _prompt_layer(x, mem, B, T, layer, p, final_gain=None):
    n_mem = mem.shape[0] // B
    kv = _norm_matmul(mem, p["norm_mem"], p["wx_kv"], layer, min(512, mem.shape[0]), 512)
    tm = min(512, T)
    proj, gates = _mix_inputs(x, layer, p, min(1024, T))
    mix_ml, c, n, m = _mlstm(proj, gates, p["bg"], p["norm_heads"], layer, B, T, min(LANES, T), None, BF16)
    mix_sb = _sb_prompt(proj, p["b_sb"], p["norm_heads"], layer, B, T)
    tpb = T // tm
    x = _mix_cross(mix_ml, mix_sb, p["w_out"], x, p["norm_cross"], p["wx_q"], p["wx_o"], layer, kv, kv,
                   pl.BlockSpec((n_mem, X_WIDTH), lambda i: (i // tpb, 0)),
                   pl.BlockSpec((n_mem, X_WIDTH), lambda i: (i // tpb, 1)), tm)
    x, st = _ffn(x, p["norm_ffn"], p["w_up"], p["conv_w"], p["conv_b"], p["w_down"], layer, B, T, tm, 512,
                 final_gain=final_gain)
    return x, (proj, c, n, m, st, kv)


def _sample_layer(x, B, T, layer, p, cache_k, cache_v, page_table, state, conv_state, mem_k, mem_v,
                  final_gain=None):
    M = B * T
    H, d = SB_HEADS, SB_DIM
    proj, gates = _mix_inputs(x, layer, p, M)
    mix_ml, c, n, m = _mlstm(proj, gates, p["bg"], p["norm_heads"], layer, B, T, T, state, F32)
    heads = lambda colblk: proj[:, colblk * SB_WIDTH:(colblk + 1) * SB_WIDTH].reshape(B, T, H, d)
    sbk, sbv = heads(COL_SK), heads(COL_SV)
    head_rows = lambda a: a.transpose(0, 2, 1, 3).reshape(B, H * T, d)
    mix_sb = _sb_sample(head_rows(heads(COL_SQ)), head_rows(sbk), head_rows(sbv), cache_k, cache_v, layer,
                        page_table, p["bias_col"], p["gn_rows"], B, T)
    mix_sb = mix_sb.reshape(B, H, T, d).transpose(0, 2, 1, 3).reshape(M, SB_WIDTH)
    x = _out_proj(mix_ml, mix_sb, p["w_out"], layer, x, M)
    n_mem = mem_k.shape[2]
    mem_spec = pl.BlockSpec((None, None, n_mem, X_WIDTH), lambda i: (layer, i, 0, 0))
    x = _cross(x, p["norm_cross"], p["wx_q"], p["wx_o"], layer, mem_k, mem_v, mem_spec, mem_spec, T)
    time_major = lambda a, n0, n1: a.reshape(n0, n1, -1).transpose(1, 0, 2).reshape(M, -1)
    x, st = _ffn(time_major(x, B, T), p["norm_ffn"], p["w_up"], p["conv_w"], p["conv_b"], p["w_down"], layer,
                 B, T, M, 512, buf=conv_state, final_gain=final_gain)
    x = time_major(x, T, B)
    return x, (sbk, sbv, c, n, m, st)


def kernel(x_prompt, x_sample, mem_prompt, cache_sb_k, cache_sb_v, state_mlstm_c, state_mlstm_n, state_mlstm_m,
           state_ffn_conv, cache_mem_k, cache_mem_v, page_table, norm_mix, w_in, b_gates, b_sb, norm_heads, w_out,
           norm_cross, norm_mem, wx_q, wx_kv, wx_o, norm_ffn, w_up, conv_w, conv_b, w_down, norm_final):
    Bp, Tp, D = x_prompt.shape
    Bs, Ts, _ = x_sample.shape
    depth = w_in.shape[0]
    n_mem = cache_mem_k.shape[2]
    n_pool = cache_sb_k.shape[1]
    p = _prep_params(Ts, w_in, b_gates, b_sb, norm_mix, norm_heads, w_out, norm_cross, norm_mem, wx_q, wx_kv, wx_o,
                     norm_ffn, w_up, conv_w, conv_b, w_down)
    state = (state_mlstm_c, state_mlstm_n, state_mlstm_m)
    conv_state = state_ffn_conv.transpose(0, 2, 1, 3)
    cache_k = cache_sb_k.reshape(depth, n_pool, PAGE * SB_HEADS, SB_DIM)
    cache_v = cache_sb_v.reshape(depth, n_pool, PAGE * SB_HEADS, SB_DIM)
    mem_k = cache_mem_k.reshape(depth, Bs, n_mem, X_WIDTH)
    mem_v = cache_mem_v.reshape(depth, Bs, n_mem, X_WIDTH)
    yp = x_prompt.reshape(Bp * Tp, D)
    ys = x_sample.reshape(Bs * Ts, D)
    mem = mem_prompt.reshape(Bp * mem_prompt.shape[1], D)
    P = [[] for _ in range(6)]
    S = [[] for _ in range(6)]
    for l in range(depth):
        gf = norm_final if l == depth - 1 else None
        yp, st_p = _prompt_layer(yp, mem, Bp, Tp, l, p, gf)
        for lst, a in zip(P, st_p):
            lst.append(a)
        ys, st_s = _sample_layer(ys, Bs, Ts, l, p, cache_k, cache_v, page_table, state, conv_state, mem_k, mem_v,
                                 gf)
        for lst, a in zip(S, st_s):
            lst.append(a)
    y_prompt = yp.reshape(Bp, Tp, D)
    y_sample = ys.reshape(Bs, Ts, D)
    projs, pc, pn, pm, pconv, kvs = P
    cols = lambda a, j: jnp.stack([x[:, j * SB_WIDTH:(j + 1) * SB_WIDTH] for x in a])
    p_sb_k = cols(projs, COL_SK).reshape(depth, Bp, Tp, SB_HEADS, SB_DIM)
    p_sb_v = cols(projs, COL_SV).reshape(depth, Bp, Tp, SB_HEADS, SB_DIM)
    kv = jnp.stack(kvs)
    p_mem_k = kv[:, :, :X_WIDTH].reshape(depth, Bp, n_mem, X_HEADS, X_DIM)
    p_mem_v = kv[:, :, X_WIDTH:].reshape(depth, Bp, n_mem, X_HEADS, X_DIM)
    return ((y_prompt, y_sample, p_sb_k, p_sb_v, jnp.stack(pc), jnp.stack(pn), jnp.stack(pm), jnp.stack(pconv),
             p_mem_k, p_mem_v) + tuple(jnp.stack(a) for a in S))
```
